```python
import math
import jax, jax.numpy as jnp
from jax import lax
import numpy as np


D_MODEL = 1024
BATCH = 2
SEQ = 8192
DEPTH = 2

N_A_LAYERS = DEPTH // 2
N_B_LAYERS = DEPTH - N_A_LAYERS
ALPHA = (2.0 * DEPTH) ** 0.25
BETA = (8.0 * DEPTH) ** -0.25
LN_EPS = 1e-5
ROPE_THETA = 10000.0
NEG = -1e30

RET_HEADS = 4
RET_DK = D_MODEL // RET_HEADS
RET_DV = 2 * RET_DK
RET_CHUNK = 128

NSA_HEADS = 16
NSA_GROUPS = 4
NSA_REP = NSA_HEADS // NSA_GROUPS
NSA_DH = D_MODEL // NSA_HEADS
KV_W = NSA_GROUPS * NSA_DH
CMP_LEN = 32
CMP_STRIDE = 16
CMP_HIDDEN = 256
SLC_BLOCK = 64
SLC_TOPK = 16
WINDOW = 512
Q_BLOCK = 128
FORCE_BONUS = 1e3

N_EXPERTS = 16
N_EXPERT_GROUPS = 4
EXPERTS_PER_GROUP = N_EXPERTS // N_EXPERT_GROUPS
TOP_K = 2
D_EXPERT = 512

kernel_name = 'hybrid_retnet_nsa_moe'


def layer_norm(x, g, b):
    xf = x.astype(jnp.float32)
    mu = xf.mean(-1, keepdims=True)
    var = jnp.square(xf - mu).mean(-1, keepdims=True)
    y = (xf - mu) * lax.rsqrt(var + LN_EPS) * g.astype(jnp.float32) + b.astype(jnp.float32)
    return y.astype(x.dtype)


def rope(x, pos):
    d = x.shape[-1]
    inv = ROPE_THETA ** (-jnp.arange(0, d, 2, dtype=jnp.float32) / d)
    ang = pos.astype(jnp.float32)[:, None] * inv[None, :]
    cos = jnp.cos(ang)[None, :, None, :]
    sin = jnp.sin(ang)[None, :, None, :]
    xf = x.astype(jnp.float32)
    x1, x2 = xf[..., : d // 2], xf[..., d // 2:]
    return jnp.concatenate([x1 * cos - x2 * sin, x2 * cos + x1 * sin], -1).astype(x.dtype)


def masked_softmax(s, mask):
    p = jax.nn.softmax(jnp.where(mask, s, NEG), axis=-1)
    return jnp.where(mask, p, 0.0)


def retention(h, w_in, w_out):
    B, S, _ = h.shape
    dt = h.dtype
    H, dk, dv, C = RET_HEADS, RET_DK, RET_DV, RET_CHUNK
    proj = h @ w_in
    q, k, v, g = jnp.split(proj, [H * dk, 2 * H * dk, 2 * H * dk + H * dv], axis=-1)
    pos = jnp.arange(S)
    q = rope(q.reshape(B, S, H, dk), pos)
    k = rope(k.reshape(B, S, H, dk), pos) * (dk ** -0.5)
    v = v.reshape(B, S, H, dv)
    nC = S // C

    def chunks(t):
        return t.reshape(B, nC, C, H, -1).transpose(0, 3, 1, 2, 4)

    qc, kc, vc = chunks(q), chunks(k), chunks(v)
    log_gamma = jnp.log1p(-(2.0 ** (-5.0 - jnp.arange(H, dtype=jnp.float32))))
    idx = jnp.arange(C, dtype=jnp.float32)
    diff = idx[:, None] - idx[None, :]
    decay_mask = jnp.where(diff >= 0, jnp.exp(log_gamma[:, None, None] * jnp.maximum(diff, 0.0)), 0.0).astype(dt)
    inner = jnp.einsum('bhncd,bhnmd->bhncm', qc, kc) * decay_mask[None, :, None]
    o_inner = jnp.einsum('bhncm,bhnme->bhnce', inner, vc)
    q_decay = jnp.exp(log_gamma[:, None] * (idx[None, :] + 1.0)).astype(dt)
    k_decay = jnp.exp(log_gamma[:, None] * (C - 1.0 - idx[None, :])).astype(dt)
    chunk_decay = jnp.exp(log_gamma * C).astype(dt)

    def step(state, xs):
        q_i, k_i, v_i = xs
        o = jnp.einsum('bhcd,bhde->bhce', q_i * q_decay[None, :, :, None], state)
        state = state * chunk_decay[None, :, None, None] + jnp.einsum(
            'bhcd,bhce->bhde', k_i * k_decay[None, :, :, None], v_i)
        return state, o

    state0 = jnp.zeros((B, H, dk, dv), dt)
    _, o_cross = lax.scan(step, state0, (qc.transpose(2, 0, 1, 3, 4),
                                         kc.transpose(2, 0, 1, 3, 4),
                                         vc.transpose(2, 0, 1, 3, 4)))
    o = o_inner + o_cross.transpose(1, 2, 0, 3, 4)
    o = o.transpose(0, 2, 3, 1, 4).reshape(B, S, H, dv)
    of = o.astype(jnp.float32)
    mu = of.mean(-1, keepdims=True)
    var = jnp.square(of - mu).mean(-1, keepdims=True)
    o = ((of - mu) * lax.rsqrt(var + LN_EPS)).reshape(B, S, H * dv).astype(dt)
    return (jax.nn.silu(g) * o) @ w_out


def nsa_shared_kv(h, w_kv, cmp_pe_k, cmp_pe_v, cmp_k_w1, cmp_k_w2, cmp_v_w1, cmp_v_w2):
    B, S, _ = h.shape
    G, dh = NSA_GROUPS, NSA_DH
    pos = jnp.arange(S)
    parts = jnp.split(h @ w_kv, 6, axis=-1)
    kc, vc, ks, vs, kw, vw = [p.reshape(B, S, G, dh) for p in parts]
    kc, ks, kw = rope(kc, pos), rope(ks, pos), rope(kw, pos)
    n_cmp = (S - CMP_LEN) // CMP_STRIDE + 1
    cidx = jnp.arange(n_cmp)[:, None] * CMP_STRIDE + jnp.arange(CMP_LEN)[None, :]

    def compress(t, pe, w1, w2):
        blk = t[:, cidx] + pe[None, None, :, None, :]
        blk = blk.transpose(0, 3, 1, 2, 4).reshape(B, G, n_cmp, CMP_LEN * dh)
        return jax.nn.gelu(blk @ w1) @ w2

    k_cmp = compress(kc, cmp_pe_k, cmp_k_w1, cmp_k_w2)
    v_cmp = compress(vc, cmp_pe_v, cmp_v_w1, cmp_v_w2)
    nS = S // SLC_BLOCK

    def to_blocks(t):
        return t.transpose(0, 2, 1, 3).reshape(B, G, nS, SLC_BLOCK, dh)

    def pad_win(t):
        return jnp.pad(t.transpose(0, 2, 1, 3), ((0, 0), (0, 0), (WINDOW, 0), (0, 0)))

    return (k_cmp, v_cmp, to_blocks(ks), to_blocks(vs), pad_win(kw), pad_win(vw))


def nsa_attention(h, w_in, w_out, shared):
    k_cmp, v_cmp, k_slc, v_slc, k_win, v_win = shared
    B, S, _ = h.shape
    dt = h.dtype
    H, G, R, dh = NSA_HEADS, NSA_GROUPS, NSA_REP, NSA_DH
    proj = h @ w_in
    q = rope(proj[..., : H * dh].reshape(B, S, H, dh), jnp.arange(S)) * (dh ** -0.5)
    gates = jax.nn.sigmoid(proj[..., H * dh:].astype(jnp.float32)).reshape(B, S, H, 3)
    nQ = S // Q_BLOCK
    q_blocks = q.reshape(B, nQ, Q_BLOCK, G, R, dh).transpose(1, 0, 3, 4, 2, 5)
    g_blocks = gates.reshape(B, nQ, Q_BLOCK, G, R, 3).transpose(1, 0, 3, 4, 2, 5)
    n_cmp = k_cmp.shape[2]
    nS = k_slc.shape[2]
    top_k = min(SLC_TOPK, nS)
    ci = jnp.arange(n_cmp)
    sj = jnp.arange(nS)
    cmp_end = ci * CMP_STRIDE + CMP_LEN - 1
    overlap = ((ci[:, None] * CMP_STRIDE < (sj[None, :] + 1) * SLC_BLOCK)
               & (ci[:, None] * CMP_STRIDE + CMP_LEN > sj[None, :] * SLC_BLOCK)).astype(jnp.float32)
    gather_blocks = jax.vmap(jax.vmap(lambda kb, ix: kb[ix]))

    def block(xs):
        i, qb, gb = xs
        t = i * Q_BLOCK + jnp.arange(Q_BLOCK)
        s = jnp.einsum('bgrqd,bgnd->bgrqn', qb, k_cmp).astype(jnp.float32)
        p_cmp = masked_softmax(s, cmp_end[None, :] <= t[:, None])
        o_cmp = jnp.einsum('bgrqn,bgnd->bgrqd', p_cmp.astype(dt), v_cmp)
        imp = jnp.einsum('bgrqn,ns->bgqs', p_cmp, overlap)
        valid = sj[None, :] * SLC_BLOCK <= t[:, None]
        cur = t // SLC_BLOCK
        forced = (sj[None, :] == 0) | (sj[None, :] == cur[:, None]) | (sj[None, :] == cur[:, None] - 1)
        score = jnp.where(valid, imp + jnp.where(forced, FORCE_BONUS, 0.0), NEG)
        _, sel = lax.top_k(score, top_k)
        ks = gather_blocks(k_slc, sel).reshape(B, G, Q_BLOCK, top_k * SLC_BLOCK, dh)
        vs = gather_blocks(v_slc, sel).reshape(B, G, Q_BLOCK, top_k * SLC_BLOCK, dh)
        key_pos = (sel[..., None] * SLC_BLOCK + jnp.arange(SLC_BLOCK)).reshape(B, G, Q_BLOCK, top_k * SLC_BLOCK)
        s = jnp.einsum('bgrqd,bgqkd->bgrqk', qb, ks).astype(jnp.float32)
        p = masked_softmax(s, (key_pos <= t[:, None])[:, :, None])
        o_slc = jnp.einsum('bgrqk,bgqkd->bgrqd', p.astype(dt), vs)
        kw = lax.dynamic_slice_in_dim(k_win, i * Q_BLOCK, Q_BLOCK + WINDOW, axis=2)
        vw = lax.dynamic_slice_in_dim(v_win, i * Q_BLOCK, Q_BLOCK + WINDOW, axis=2)
        wpos = i * Q_BLOCK - WINDOW + jnp.arange(Q_BLOCK + WINDOW)
        dist = t[:, None] - wpos[None, :]
        wmask = (dist >= 0) & (dist < WINDOW) & (wpos[None, :] >= 0)
        s = jnp.einsum('bgrqd,bgkd->bgrqk', qb, kw).astype(jnp.float32)
        p = masked_softmax(s, wmask)
        o_win = jnp.einsum('bgrqk,bgkd->bgrqd', p.astype(dt), vw)
        o = gb[..., 0:1] * o_cmp + gb[..., 1:2] * o_slc + gb[..., 2:3] * o_win
        return o.astype(dt)

    o = lax.map(block, (jnp.arange(nQ), q_blocks, g_blocks))
    o = o.transpose(1, 0, 4, 2, 3, 5).reshape(B, S, H * dh)
    return o @ w_out


def moe(h, router_w, router_b, w_gate, w_up, w_down):
    B, S, D = h.shape
    x = h.reshape(-1, D)
    T = x.shape[0]
    aff = jax.nn.sigmoid((x @ router_w).astype(jnp.float32))
    sel_score = aff + router_b.astype(jnp.float32)
    grp_score = lax.top_k(sel_score.reshape(T, N_EXPERT_GROUPS, EXPERTS_PER_GROUP), TOP_K)[0].sum(-1)
    best = jnp.argmax(grp_score, axis=-1)
    in_group = (jnp.arange(N_EXPERTS) // EXPERTS_PER_GROUP)[None, :] == best[:, None]
    _, top_idx = lax.top_k(jnp.where(in_group, sel_score, NEG), TOP_K)
    chosen = jax.nn.one_hot(top_idx, N_EXPERTS, dtype=jnp.float32).sum(1)
    w = aff * chosen
    w = (w / w.sum(-1, keepdims=True)).astype(x.dtype)
    y = jnp.zeros_like(x)
    for e in range(N_EXPERTS):
        he = (jax.nn.silu(x @ w_gate[e]) * (x @ w_up[e])) @ w_down[e]
        y = y + w[:, e:e + 1] * he
    return y.reshape(B, S, D)


def setup_inputs(seed: int = 0) -> dict:
    key = jax.random.key(seed)
    ks = jax.random.split(key, 32)
    f32 = jnp.float32
    D = D_MODEL

    def nrm(k, shape, scale):
        return jax.random.normal(k, shape, f32) * scale

    x = nrm(ks[0], (BATCH, SEQ, D), 1.0)
    ret_w_in = jnp.concatenate([
        nrm(ks[1], (N_A_LAYERS, D, 2 * RET_HEADS * RET_DK), D ** -0.5),
        nrm(ks[2], (N_A_LAYERS, D, RET_HEADS * RET_DV), D ** -0.5 * BETA),
        nrm(ks[3], (N_A_LAYERS, D, RET_HEADS * RET_DV), D ** -0.5)], axis=-1)
    ret_w_out = nrm(ks[4], (N_A_LAYERS, RET_HEADS * RET_DV, D), (RET_HEADS * RET_DV) ** -0.5 * BETA)
    kv_k = nrm(ks[5], (3, D, KV_W), D ** -0.5)
    kv_v = nrm(ks[6], (3, D, KV_W), D ** -0.5 * BETA)
    nsa_w_kv = jnp.stack([kv_k, kv_v], 1).transpose(2, 0, 1, 3).reshape(D, 6 * KV_W)
    cmp_pe_k = nrm(ks[7], (CMP_LEN, NSA_DH), 0.1)
    cmp_pe_v = nrm(ks[8], (CMP_LEN, NSA_DH), 0.1)
    cmp_k_w1 = nrm(ks[9], (CMP_LEN * NSA_DH, CMP_HIDDEN), (CMP_LEN * NSA_DH) ** -0.5)
    cmp_k_w2 = nrm(ks[10], (CMP_HIDDEN, NSA_DH), CMP_HIDDEN ** -0.5)
    cmp_v_w1 = nrm(ks[11], (CMP_LEN * NSA_DH, CMP_HIDDEN), (CMP_LEN * NSA_DH) ** -0.5)
    cmp_v_w2 = nrm(ks[12], (CMP_HIDDEN, NSA_DH), CMP_HIDDEN ** -0.5)
    nsa_w_in = nrm(ks[13], (N_B_LAYERS, D, NSA_HEADS * NSA_DH + 3 * NSA_HEADS), D ** -0.5)
    nsa_w_out = nrm(ks[14], (N_B_LAYERS, NSA_HEADS * NSA_DH, D), (NSA_HEADS * NSA_DH) ** -0.5 * BETA)
    router_w = nrm(ks[15], (D, N_EXPERTS), D ** -0.5)
    router_b = nrm(ks[16], (N_EXPERTS,), 0.01)
    moe_w_gate = nrm(ks[17], (DEPTH, N_EXPERTS, D, D_EXPERT), D ** -0.5)
    moe_w_up = nrm(ks[18], (DEPTH, N_EXPERTS, D, D_EXPERT), D ** -0.5 * BETA)
    moe_w_down = nrm(ks[19], (DEPTH, N_EXPERTS, D_EXPERT, D), D_EXPERT ** -0.5 * BETA)
    ln_mix_g = 1.0 + nrm(ks[20], (DEPTH, D), 0.02)
    ln_mix_b = nrm(ks[21], (DEPTH, D), 0.02)
    ln_ffn_g = 1.0 + nrm(ks[22], (DEPTH, D), 0.02)
    ln_ffn_b = nrm(ks[23], (DEPTH, D), 0.02)
    return {'x': x, 'ret_w_in': ret_w_in, 'ret_w_out': ret_w_out, 'nsa_w_kv': nsa_w_kv,
            'cmp_pe_k': cmp_pe_k, 'cmp_pe_v': cmp_pe_v, 'cmp_k_w1': cmp_k_w1, 'cmp_k_w2': cmp_k_w2,
            'cmp_v_w1': cmp_v_w1, 'cmp_v_w2': cmp_v_w2, 'nsa_w_in': nsa_w_in, 'nsa_w_out': nsa_w_out,
            'router_w': router_w, 'router_b': router_b, 'moe_w_gate': moe_w_gate, 'moe_w_up': moe_w_up,
            'moe_w_down': moe_w_down, 'ln_mix_g': ln_mix_g, 'ln_mix_b': ln_mix_b,
            'ln_ffn_g': ln_ffn_g, 'ln_ffn_b': ln_ffn_b}


def reference(x, ret_w_in, ret_w_out, nsa_w_kv, cmp_pe_k, cmp_pe_v, cmp_k_w1, cmp_k_w2, cmp_v_w1, cmp_v_w2,
              nsa_w_in, nsa_w_out, router_w, router_b, moe_w_gate, moe_w_up, moe_w_down,
              ln_mix_g, ln_mix_b, ln_ffn_g, ln_ffn_b):
    h = x
    shared = None
    for l in range(DEPTH):
        if l < N_A_LAYERS:
            mix = retention(h, ret_w_in[l], ret_w_out[l])
        else:
            if l == N_A_LAYERS:
                shared = nsa_shared_kv(h, nsa_w_kv, cmp_pe_k, cmp_pe_v, cmp_k_w1, cmp_k_w2, cmp_v_w1, cmp_v_w2)
            j = l - N_A_LAYERS
            mix = nsa_attention(h, nsa_w_in[j], nsa_w_out[j], shared)
        h = layer_norm(ALPHA * h + mix, ln_mix_g[l], ln_mix_b[l])
        h = layer_norm(ALPHA * h + moe(h, router_w, router_b, moe_w_gate[l], moe_w_up[l], moe_w_down[l]),
                       ln_ffn_g[l], ln_ffn_b[l])
    return h
```

```python
import functools
import math

import jax
import jax.numpy as jnp
import numpy as np
from jax import lax
from jax.experimental import pallas as pl
from jax.experimental.pallas import tpu as pltpu

F32 = jnp.float32
BF16 = jnp.bfloat16

D_MODEL = 1024
DEPTH = 2
ALPHA = (2.0 * DEPTH) ** 0.25
LN_EPS = 1e-5
ROPE_THETA = 10000.0
NEG = -1e30

RET_HEADS = 4
RET_DK = D_MODEL // RET_HEADS
RET_DV = 2 * RET_DK
RET_CHUNK = 256

NSA_HEADS = 16
NSA_GROUPS = 4
NSA_REP = NSA_HEADS // NSA_GROUPS
NSA_DH = D_MODEL // NSA_HEADS
KV_W = NSA_GROUPS * NSA_DH
CMP_LEN = 32
CMP_STRIDE = 16
CMP_HIDDEN = 256
SLC_BLOCK = 64
SLC_TOPK = 16
WINDOW = 512
Q_BLOCK = 128
FORCE_BONUS = 1e3
SLC_CHUNK = 512

N_EXPERTS = 16
N_EXPERT_GROUPS = 4
EXPERTS_PER_GROUP = N_EXPERTS // N_EXPERT_GROUPS
D_EXPERT = 512

LANES = 128
ROW_TILE = 512
MOE_TILE = 256
VMEM_LIMIT = 56 * 1024 * 1024


def _cparams(sem):
    return pltpu.CompilerParams(dimension_semantics=sem, vmem_limit_bytes=VMEM_LIMIT)


def _nt_dot(a, b):
    return lax.dot_general(a, b, (((1,), (1,)), ((), ())), preferred_element_type=F32)


def _dot(a, b):
    return jnp.dot(a, b, preferred_element_type=F32)


def _rope_tables_half(seq, d):
    inv = ROPE_THETA ** (-jnp.arange(0, d, 2, dtype=F32) / d)
    ang = jnp.arange(seq, dtype=F32)[:, None] * inv[None, :]
    return jnp.cos(ang), jnp.sin(ang)


def _rope_tables_packed(seq, d, width):
    cos, sin = _rope_tables_half(seq, d)
    zero = jnp.zeros_like(sin)
    reps = width // d
    cos_t = jnp.tile(jnp.concatenate([cos, cos], -1), (1, reps))
    sin_lo = jnp.tile(jnp.concatenate([-sin, zero], -1), (1, reps))
    sin_hi = jnp.tile(jnp.concatenate([zero, sin], -1), (1, reps))
    return cos_t, sin_lo, sin_hi


def _ret_proj_kernel(x_ref, w_ref, cos_ref, sin_ref, o_ref, *, n_rope_tiles, k_scale):
    j = pl.program_id(1)
    acc = _dot(x_ref[...].astype(BF16), w_ref[...])
    tn = acc.shape[1]
    half = RET_DK // 2

    @pl.when(j < n_rope_tiles)
    def _():
        c = cos_ref[...]
        s = sin_ref[...]
        scale = jnp.where(j >= n_rope_tiles // 2, k_scale, 1.0).astype(F32)
        for hh in range(tn // RET_DK):
            lo = hh * RET_DK
            x1 = acc[:, lo:lo + half]
            x2 = acc[:, lo + half:lo + RET_DK]
            o_ref[:, lo:lo + half] = ((x1 * c - x2 * s) * scale).astype(o_ref.dtype)
            o_ref[:, lo + half:lo + RET_DK] = ((x2 * c + x1 * s) * scale).astype(o_ref.dtype)

    @pl.when(j >= n_rope_tiles)
    def _():
        o_ref[...] = acc.astype(o_ref.dtype)


def _ret_proj(x2d, w_bf16, seq):
    t, d = x2d.shape
    n = w_bf16.shape[1]
    tm, tn = ROW_TILE, 1024
    cos, sin = _rope_tables_half(seq, RET_DK)
    tiles_per_seq = seq // tm
    n_rope_tiles = 2 * RET_HEADS * RET_DK // tn
    kern = functools.partial(_ret_proj_kernel, n_rope_tiles=n_rope_tiles, k_scale=RET_DK ** -0.5)
    return pl.pallas_call(
        kern,
        grid=(t // tm, n // tn),
        in_specs=[
            pl.BlockSpec((tm, d), lambda i, j: (i, 0)),
            pl.BlockSpec((d, tn), lambda i, j: (0, j)),
            pl.BlockSpec((tm, RET_DK // 2), lambda i, j: (i % tiles_per_seq, 0)),
            pl.BlockSpec((tm, RET_DK // 2), lambda i, j: (i % tiles_per_seq, 0)),
        ],
        out_specs=pl.BlockSpec((tm, tn), lambda i, j: (i, j)),
        out_shape=jax.ShapeDtypeStruct((t, n), BF16),
        compiler_params=_cparams(("parallel", "arbitrary")),
        name="ret_proj",
    )(x2d, w_bf16, cos, sin)


def _ret_core_kernel(q_ref, k_ref, v_ref, g_ref, mask_ref, qd_ref, kd_ref, cd_ref, o_ref, state_ref):
    c = pl.program_id(2)

    @pl.when(c == 0)
    def _():
        state_ref[...] = jnp.zeros_like(state_ref)

    q = q_ref[...]
    k = k_ref[...]
    v = v_ref[...]
    inner = _nt_dot(q, k) * mask_ref[0]
    state = state_ref[...]
    qs = (q.astype(F32) * qd_ref[0]).astype(BF16)
    o = _dot(inner.astype(BF16), v) + _dot(qs, state.astype(BF16))
    ks = k.astype(F32) * kd_ref[0]
    state_ref[...] = state * cd_ref[0] + _dot(ks.T.astype(BF16), v)

    mu = jnp.mean(o, axis=-1, keepdims=True)
    var = jnp.mean(jnp.square(o - mu), axis=-1, keepdims=True)
    on = (o - mu) * lax.rsqrt(var + LN_EPS)
    g = g_ref[...].astype(F32)
    o_ref[...] = (g * jax.nn.sigmoid(g) * on).astype(o_ref.dtype)


def _ret_core(proj, batch, seq):
    t = proj.shape[0]
    h, dk, dv, c = RET_HEADS, RET_DK, RET_DV, min(RET_CHUNK, seq)
    nc = seq // c
    log_gamma = jnp.log1p(-(2.0 ** (-5.0 - jnp.arange(h, dtype=F32))))
    idx = jnp.arange(c, dtype=F32)
    diff = idx[:, None] - idx[None, :]
    mask = jnp.where(diff >= 0, jnp.exp(log_gamma[:, None, None] * jnp.maximum(diff, 0.0)), 0.0)
    qd = jnp.broadcast_to(jnp.exp(log_gamma[:, None] * (idx[None, :] + 1.0))[:, :, None], (h, c, dk))
    kd = jnp.broadcast_to(jnp.exp(log_gamma[:, None] * (c - 1.0 - idx[None, :]))[:, :, None], (h, c, dk))
    cd = jnp.broadcast_to(jnp.exp(log_gamma * c)[:, None, None], (h, 1, dv))
    v_off = 2 * h * dk // dv
    g_off = v_off + h
    return pl.pallas_call(
        _ret_core_kernel,
        grid=(batch, h, nc),
        in_specs=[
            pl.BlockSpec((c, dk), lambda b, hh, cc: (b * nc + cc, hh)),
            pl.BlockSpec((c, dk), lambda b, hh, cc: (b * nc + cc, h + hh)),
            pl.BlockSpec((c, dv), lambda b, hh, cc: (b * nc + cc, v_off + hh)),
            pl.BlockSpec((c, dv), lambda b, hh, cc: (b * nc + cc, g_off + hh)),
            pl.BlockSpec((1, c, c), lambda b, hh, cc: (hh, 0, 0)),
            pl.BlockSpec((1, c, dk), lambda b, hh, cc: (hh, 0, 0)),
            pl.BlockSpec((1, c, dk), lambda b, hh, cc: (hh, 0, 0)),
            pl.BlockSpec((1, 1, dv), lambda b, hh, cc: (hh, 0, 0)),
        ],
        out_specs=pl.BlockSpec((c, dv), lambda b, hh, cc: (b * nc + cc, hh)),
        out_shape=jax.ShapeDtypeStruct((t, h * dv), BF16),
        scratch_shapes=[pltpu.VMEM((dk, dv), F32)],
        compiler_params=_cparams(("parallel", "parallel", "arbitrary")),
        name="ret_core",
    )(proj, proj, proj, proj, mask, qd, kd, cd)


def _layer_norm_rows(y, g, b):
    mu = jnp.mean(y, axis=-1, keepdims=True)
    var = jnp.mean(jnp.square(y - mu), axis=-1, keepdims=True)
    return (y - mu) * lax.rsqrt(var + LN_EPS) * g + b


def _split_bf16(x):
    hi = x.astype(BF16)
    lo = (x - hi.astype(F32)).astype(BF16)
    return hi, lo


def _route(hn, rwt_hi_ref, rwt_lo_ref, rb_ref, ridx_ref, rw_ref):
    h_hi, h_lo = _split_bf16(hn)
    w_hi = rwt_hi_ref[...]
    logits = _nt_dot(w_hi, h_hi) + _nt_dot(w_hi, h_lo) + _nt_dot(rwt_lo_ref[...], h_hi)
    aff = jax.nn.sigmoid(logits)
    sel = aff + rb_ref[...]
    tm = hn.shape[0]

    def row(a, e):
        return a[e:e + 1, :]

    best_g = jnp.zeros((1, tm), jnp.int32)
    best_s = None
    for gi in range(N_EXPERT_GROUPS):
        a0, a1, a2, a3 = [row(sel, EXPERTS_PER_GROUP * gi + e) for e in range(EXPERTS_PER_GROUP)]
        hi01, lo01 = jnp.maximum(a0, a1), jnp.minimum(a0, a1)
        hi23, lo23 = jnp.maximum(a2, a3), jnp.minimum(a2, a3)
        top2 = jnp.maximum(hi01, hi23) + jnp.maximum(jnp.minimum(hi01, hi23), jnp.maximum(lo01, lo23))
        if best_s is None:
            best_s = top2
        else:
            upd = top2 > best_s
            best_g = jnp.where(upd, gi, best_g)
            best_s = jnp.where(upd, top2, best_s)

    def pick(a, e):
        out = row(a, e)
        for gi in range(1, N_EXPERT_GROUPS):
            out = jnp.where(best_g == gi, row(a, EXPERTS_PER_GROUP * gi + e), out)
        return out

    cs = [pick(sel, e) for e in range(EXPERTS_PER_GROUP)]
    af = [pick(aff, e) for e in range(EXPERTS_PER_GROUP)]
    i1 = jnp.zeros((1, tm), jnp.int32)
    s1, a1v = cs[0], af[0]
    for e in range(1, EXPERTS_PER_GROUP):
        upd = cs[e] > s1
        i1 = jnp.where(upd, e, i1)
        s1 = jnp.where(upd, cs[e], s1)
        a1v = jnp.where(upd, af[e], a1v)
    i2 = jnp.full((1, tm), -1, jnp.int32)
    s2 = jnp.full((1, tm), -jnp.inf, F32)
    a2v = jnp.zeros((1, tm), F32)
    for e in range(EXPERTS_PER_GROUP):
        upd = (i1 != e) & ((i2 < 0) | (cs[e] > s2))
        i2 = jnp.where(upd, e, i2)
        s2 = jnp.where(upd, cs[e], s2)
        a2v = jnp.where(upd, af[e], a2v)
    tot = a1v + a2v
    base = best_g * EXPERTS_PER_GROUP
    zi = jnp.zeros((6, tm), jnp.int32)
    zf = jnp.zeros((6, tm), F32)
    ridx_ref[...] = jnp.concatenate([base + i1, base + i2, zi], axis=0)
    rw_ref[...] = jnp.concatenate([a1v / tot, a2v / tot, zf], axis=0)


def _mix_out_kernel(a_ref, w_ref, h_ref, g_ref, b_ref, rwt_hi_ref, rwt_lo_ref, rb_ref,
                    o_ref, ridx_ref, rw_ref):
    mix = _dot(a_ref[...], w_ref[...])
    hn = _layer_norm_rows(ALPHA * h_ref[...] + mix, g_ref[...], b_ref[...])
    o_ref[...] = hn
    _route(hn, rwt_hi_ref, rwt_lo_ref, rb_ref, ridx_ref, rw_ref)


def _mix_out(a_bf16, w_bf16, h2d, ln_g, ln_b, rwt_hi, rwt_lo, rb):
    t, d = h2d.shape
    k = a_bf16.shape[1]
    tm = ROW_TILE
    row = lambda i: (i, 0)
    fixed = lambda i: (0, 0)
    return pl.pallas_call(
        _mix_out_kernel,
        grid=(t // tm,),
        in_specs=[
            pl.BlockSpec((tm, k), row),
            pl.BlockSpec((k, d), fixed),
            pl.BlockSpec((tm, d), row),
            pl.BlockSpec((1, d), fixed),
            pl.BlockSpec((1, d), fixed),
            pl.BlockSpec((N_EXPERTS, d), fixed),
            pl.BlockSpec((N_EXPERTS, d), fixed),
            pl.BlockSpec((N_EXPERTS, 1), fixed),
        ],
        out_specs=[
            pl.BlockSpec((tm, d), row),
            pl.BlockSpec((8, tm), lambda i: (0, i)),
            pl.BlockSpec((8, tm), lambda i: (0, i)),
        ],
        out_shape=[
            jax.ShapeDtypeStruct((t, d), F32),
            jax.ShapeDtypeStruct((8, t), jnp.int32),
            jax.ShapeDtypeStruct((8, t), F32),
        ],
        compiler_params=_cparams(("parallel",)),
        name="mix_out",
    )(a_bf16, w_bf16, h2d, ln_g, ln_b, rwt_hi, rwt_lo, rb)


def _expert_kernel(te_ref, rt_ref, nused_ref, h_hbm, wg_ref, wu_ref, wd_ref, o_ref, xbuf, sem):
    i = pl.program_id(0)
    tile = xbuf.shape[0]

    def row_copy(r):
        tok = rt_ref[i * tile + r]
        return pltpu.make_async_copy(h_hbm.at[pl.ds(tok, 1)], xbuf.at[pl.ds(r, 1)], sem)

    @pl.when(i < nused_ref[0])
    def _():
        def start(r, carry):
            row_copy(r).start()
            return carry

        def wait(r, carry):
            row_copy(r).wait()
            return carry

        lax.fori_loop(0, tile, start, 0)
        lax.fori_loop(0, tile, wait, 0)
        x = xbuf[...].astype(BF16)
        gate = _dot(x, wg_ref[0])
        up = _dot(x, wu_ref[0])
        act = (gate * jax.nn.sigmoid(gate) * up).astype(BF16)
        o_ref[...] = _dot(act, wd_ref[0])

    @pl.when(i >= nused_ref[0])
    def _():
        o_ref[...] = jnp.zeros_like(o_ref)


def _experts(h1, tile_expert, row_token, n_used, wg, wu, wd):
    t, d = h1.shape
    p = row_token.shape[0]
    n_tiles = p // MOE_TILE
    de = wg.shape[2]
    grid_spec = pltpu.PrefetchScalarGridSpec(
        num_scalar_prefetch=3,
        grid=(n_tiles,),
        in_specs=[
            pl.BlockSpec(memory_space=pl.ANY),
            pl.BlockSpec((1, d, de), lambda i, te, rt, nu: (te[i], 0, 0)),
            pl.BlockSpec((1, d, de), lambda i, te, rt, nu: (te[i], 0, 0)),
            pl.BlockSpec((1, de, d), lambda i, te, rt, nu: (te[i], 0, 0)),
        ],
        out_specs=pl.BlockSpec((MOE_TILE, d), lambda i, te, rt, nu: (i, 0)),
        scratch_shapes=[pltpu.VMEM((MOE_TILE, d), F32), pltpu.SemaphoreType.DMA(())],
    )
    return pl.pallas_call(
        _expert_kernel,
        grid_spec=grid_spec,
        out_shape=jax.ShapeDtypeStruct((p, d), F32),
        compiler_params=_cparams(("arbitrary",)),
        name="moe_experts",
    )(tile_expert, row_token, n_used, h1, wg, wu, wd)


def _combine_kernel(p0_ref, p1_ref, ys_hbm, h_ref, w0_ref, w1_ref, g_ref, b_ref, o_ref, buf0, buf1, sem):
    i = pl.program_id(0)
    tm = buf0.shape[0]

    def copies(r):
        c0 = pltpu.make_async_copy(ys_hbm.at[pl.ds(p0_ref[i * tm + r], 1)], buf0.at[pl.ds(r, 1)], sem.at[0])
        c1 = pltpu.make_async_copy(ys_hbm.at[pl.ds(p1_ref[i * tm + r], 1)], buf1.at[pl.ds(r, 1)], sem.at[1])
        return c0, c1

    def start(r, carry):
        c0, c1 = copies(r)
        c0.start()
        c1.start()
        return carry

    def wait(r, carry):
        c0, c1 = copies(r)
        c0.wait()
        c1.wait()
        return carry

    lax.fori_loop(0, tm, start, 0)
    lax.fori_loop(0, tm, wait, 0)
    y = ALPHA * h_ref[...] + (w0_ref[...] * buf0[...] + w1_ref[...] * buf1[...])
    o_ref[...] = _layer_norm_rows(y, g_ref[...], b_ref[...])


def _combine(ys, h1, pos0, pos1, w0, w1, ln_g, ln_b):
    t, d = h1.shape
    tm = MOE_TILE
    row = lambda i, p0, p1: (i, 0)
    fixed = lambda i, p0, p1: (0, 0)
    grid_spec = pltpu.PrefetchScalarGridSpec(
        num_scalar_prefetch=2,
        grid=(t // tm,),
        in_specs=[
            pl.BlockSpec(memory_space=pl.ANY),
            pl.BlockSpec((tm, d), row),
            pl.BlockSpec((tm, 1), row),
            pl.BlockSpec((tm, 1), row),
            pl.BlockSpec((1, d), fixed),
            pl.BlockSpec((1, d), fixed),
        ],
        out_specs=pl.BlockSpec((tm, d), row),
        scratch_shapes=[pltpu.VMEM((tm, d), F32), pltpu.VMEM((tm, d), F32), pltpu.SemaphoreType.DMA((2,))],
    )
    return pl.pallas_call(
        _combine_kernel,
        grid_spec=grid_spec,
        out_shape=jax.ShapeDtypeStruct((t, d), F32),
        compiler_params=_cparams(("arbitrary",)),
        name="moe_combine",
    )(pos0, pos1, ys, h1, w0, w1, ln_g, ln_b)


def _moe_plan(ridx, rw):
    t = ridx.shape[1]
    e = jnp.concatenate([ridx[0], ridx[1]])
    onehot = (e[:, None] == jnp.arange(N_EXPERTS)[None, :]).astype(jnp.int32)
    csum = jnp.cumsum(onehot, axis=0)
    rank = jnp.take_along_axis(csum, e[:, None], axis=1)[:, 0] - 1
    counts = csum[-1]
    padded = (counts + MOE_TILE - 1) // MOE_TILE * MOE_TILE
    ends = jnp.cumsum(padded)
    pos = (ends - padded)[e] + rank
    p = 2 * t + N_EXPERTS * MOE_TILE
    tok = jnp.concatenate([jnp.arange(t, dtype=jnp.int32)] * 2)
    row_token = jnp.zeros((p,), jnp.int32).at[pos].set(tok)
    tile_start = jnp.arange(p // MOE_TILE, dtype=jnp.int32) * MOE_TILE
    tile_expert = jnp.minimum(jnp.searchsorted(ends, tile_start, side="right"), N_EXPERTS - 1).astype(jnp.int32)
    n_used = (ends[-1] // MOE_TILE).astype(jnp.int32).reshape(1)
    return (tile_expert, row_token, n_used, pos[:t].astype(jnp.int32), pos[t:].astype(jnp.int32),
            rw[0][:, None], rw[1][:, None])


def _nsa_proj_kernel(x_ref, w_ref, cos_ref, slo_ref, shi_ref, o_ref, *, n_kv_tiles, q_scale):
    j = pl.program_id(1)
    acc = _dot(x_ref[...].astype(BF16), w_ref[...])
    tn = acc.shape[1]
    half = NSA_DH // 2
    is_q = j >= n_kv_tiles
    roped = is_q | (j % 2 == 0)

    @pl.when(roped)
    def _():
        rot = (pltpu.roll(acc, tn - half, axis=1) * slo_ref[...] + pltpu.roll(acc, half, axis=1) * shi_ref[...])
        scale = jnp.where(is_q, q_scale, 1.0).astype(F32)
        o_ref[...] = ((acc * cos_ref[...] + rot) * scale).astype(o_ref.dtype)

    @pl.when(jnp.logical_not(roped))
    def _():
        o_ref[...] = acc.astype(o_ref.dtype)


def _nsa_proj(x2d, w_bf16, seq):
    t, d = x2d.shape
    n = w_bf16.shape[1]
    tm, tn = ROW_TILE, KV_W
    cos_t, sin_lo, sin_hi = _rope_tables_packed(seq, NSA_DH, tn)
    tiles_per_seq = seq // tm
    tab = lambda i, j: (i % tiles_per_seq, 0)
    kern = functools.partial(_nsa_proj_kernel, n_kv_tiles=6, q_scale=NSA_DH ** -0.5)
    return pl.pallas_call(
        kern,
        grid=(t // tm, n // tn),
        in_specs=[
            pl.BlockSpec((tm, d), lambda i, j: (i, 0)),
            pl.BlockSpec((d, tn), lambda i, j: (0, j)),
            pl.BlockSpec((tm, tn), tab),
            pl.BlockSpec((tm, tn), tab),
            pl.BlockSpec((tm, tn), tab),
        ],
        out_specs=pl.BlockSpec((tm, tn), lambda i, j: (i, j)),
        out_shape=jax.ShapeDtypeStruct((t, n), BF16),
        compiler_params=_cparams(("parallel", "arbitrary")),
        name="nsa_proj",
    )(x2d, w_bf16, cos_t, sin_lo, sin_hi)


def _gate_kernel(x_ref, w_ref, o_ref):
    o_ref[...] = jax.nn.sigmoid(_dot(x_ref[...].astype(BF16), w_ref[...]))


def _nsa_gates(x2d, wg_bf16):
    t, d = x2d.shape
    tm = ROW_TILE
    return pl.pallas_call(
        _gate_kernel,
        grid=(t // tm,),
        in_specs=[pl.BlockSpec((tm, d), lambda i: (i, 0)), pl.BlockSpec((d, LANES), lambda i: (0, 0))],
        out_specs=pl.BlockSpec((tm, LANES), lambda i: (i, 0)),
        out_shape=jax.ShapeDtypeStruct((t, LANES), F32),
        compiler_params=_cparams(("parallel",)),
        name="nsa_gates",
    )(x2d, wg_bf16)


def _gelu_tanh(x):
    return 0.5 * x * (1.0 + jnp.tanh(math.sqrt(2.0 / math.pi) * (x + 0.044715 * (x * x * x))))


def _cmp_kernel(x_ref, pe_ref, w1_ref, w2_ref, o_ref):
    x = x_ref[0, 0, 0].astype(F32)
    n = x.shape[0]
    xa = (x + pe_ref[0, 0:1, :]).astype(BF16)
    xb = (x + pe_ref[0, 1:2, :]).astype(BF16)
    u = _dot(xa, w1_ref[0, 0])
    v = _dot(xb, w1_ref[0, 1])
    pre = u + pltpu.roll(v, n - 1, axis=0)
    o_ref[0, 0, 0] = _dot(_gelu_tanh(pre).astype(BF16), w2_ref[0]).astype(o_ref.dtype)


def _compress(xkv, pe, w1, w2):
    _, b, g, n, w = xkv.shape
    return pl.pallas_call(
        _cmp_kernel,
        grid=(2, b, g),
        in_specs=[
            pl.BlockSpec((1, 1, 1, n, w), lambda s, bb, gg: (s, bb, gg, 0, 0)),
            pl.BlockSpec((1, 2, w), lambda s, bb, gg: (s, 0, 0)),
            pl.BlockSpec((1, 2, w, CMP_HIDDEN), lambda s, bb, gg: (s, 0, 0, 0)),
            pl.BlockSpec((1, CMP_HIDDEN, NSA_DH), lambda s, bb, gg: (s, 0, 0)),
        ],
        out_specs=pl.BlockSpec((1, 1, 1, n, NSA_DH), lambda s, bb, gg: (s, bb, gg, 0, 0)),
        out_shape=jax.ShapeDtypeStruct((2, b, g, n, NSA_DH), BF16),
        compiler_params=_cparams(("parallel", "parallel", "parallel")),
        name="nsa_compress",
    )(xkv, pe, w1, w2)


def _masked_softmax_rows(s, mask):
    sm = jnp.where(mask, s, NEG)
    m = jnp.max(sm, axis=-1, keepdims=True)
    e = jnp.where(mask, jnp.exp(sm - m), 0.0)
    den = jnp.sum(e, axis=-1, keepdims=True)
    return e / jnp.where(den > 0.0, den, 1.0)


def _nsa_attn_kernel(q_ref, gate_ref, kc_ref, vc_ref, ks_ref, vs_ref, kw_ref, vw_ref, ovl_ref, exp_ref,
                     o_ref, *, seq, top_k):
    i = pl.program_id(2)
    qb = Q_BLOCK
    rep = NSA_REP
    width = rep * NSA_DH
    t0 = i * qb
    q2 = q_ref[...]
    lane_head = lax.broadcasted_iota(jnp.int32, (qb, width), 1) // NSA_DH
    zero_q = jnp.zeros_like(q2)
    q4 = jnp.concatenate([jnp.where(lane_head == r, q2, zero_q) for r in range(rep)], axis=0)
    t_row = t0 + lax.broadcasted_iota(jnp.int32, (qb, 1), 0)

    def per_head(x):
        return [x[r * qb:(r + 1) * qb] for r in range(rep)]

    n_cmp_pad = kc_ref.shape[2]
    s_c = per_head(_nt_dot(q4, kc_ref[0, 0]))
    cmp_end = lax.broadcasted_iota(jnp.int32, (qb, n_cmp_pad), 1) * CMP_STRIDE + (CMP_LEN - 1)
    mask_c = cmp_end <= t_row
    p_c = [_masked_softmax_rows(s, mask_c) for s in s_c]
    vc = vc_ref[0, 0]
    o_cmp = [_dot(p.astype(BF16), vc) for p in p_c]

    p_sum = p_c[0]
    for r in range(1, rep):
        p_sum = p_sum + p_c[r]
    p_hi, p_lo = _split_bf16(p_sum)
    ovl = ovl_ref[...]
    imp = _nt_dot(ovl, p_hi) + _nt_dot(ovl, p_lo)
    nsel = imp.shape[0]
    blk = lax.broadcasted_iota(jnp.int32, (nsel, qb), 0)
    t_col = t0 + lax.broadcasted_iota(jnp.int32, (nsel, qb), 1)
    cur = t_col // SLC_BLOCK
    valid = (blk * SLC_BLOCK <= t_col) & (blk < seq // SLC_BLOCK)
    forced = (blk == 0) | (blk == cur) | (blk == cur - 1)
    score = jnp.where(valid, imp + jnp.where(forced, FORCE_BONUS, 0.0), NEG)
    blk_f = blk.astype(F32)
    chosen = jnp.zeros((nsel, qb), F32)
    for _ in range(top_k):
        m = jnp.max(score, axis=0, keepdims=True)
        first = jnp.min(jnp.where(score == m, blk_f, float(nsel)), axis=0, keepdims=True)
        hit = blk_f == first
        chosen = jnp.where(hit, 1.0, chosen)
        score = jnp.where(hit, -jnp.inf, score)
    chosen = jnp.where(valid, chosen, 0.0)
    sel = chosen.T.astype(BF16)

    kc_len = exp_ref.shape[2]
    n_chunks = (t0 + qb - 1) // kc_len + 1

    def slc_step(c, carry):
        ms, ls, accs = carry
        start = pl.multiple_of(c * kc_len, kc_len)
        k_blk = ks_ref[0, 0, pl.ds(start, kc_len), :]
        v_blk = vs_ref[0, 0, pl.ds(start, kc_len), :]
        s_all = per_head(_nt_dot(q4, k_blk))
        picked = _dot(sel, exp_ref[c]) > 0.5
        kpos = start + lax.broadcasted_iota(jnp.int32, (qb, kc_len), 1)
        mask = picked & (kpos <= t_row)
        new_ms, new_ls, new_accs = [], [], []
        for r in range(rep):
            sm = jnp.where(mask, s_all[r], NEG)
            m_new = jnp.maximum(ms[r], jnp.max(sm, axis=-1, keepdims=True))
            p = jnp.where(mask, jnp.exp(sm - m_new), 0.0)
            a = jnp.exp(ms[r] - m_new)
            new_ms.append(m_new)
            new_ls.append(a * ls[r] + jnp.sum(p, axis=-1, keepdims=True))
            new_accs.append(a * accs[r] + _dot(p.astype(BF16), v_blk))
        return tuple(new_ms), tuple(new_ls), tuple(new_accs)

    init = (tuple(jnp.full((qb, 1), NEG, F32) for _ in range(rep)),
            tuple(jnp.zeros((qb, 1), F32) for _ in range(rep)),
            tuple(jnp.zeros((qb, width), F32) for _ in range(rep)))
    _, ls, accs = lax.fori_loop(0, n_chunks, slc_step, init)
    o_slc = [accs[r] / jnp.where(ls[r] > 0.0, ls[r], 1.0) for r in range(rep)]

    wk = kw_ref.shape[2] if kw_ref.shape[2] < qb + WINDOW else qb + WINDOW
    w_start = pl.multiple_of(jnp.maximum(t0 + qb - wk, 0), qb)
    kw = kw_ref[0, 0, pl.ds(w_start, wk), :]
    vw = vw_ref[0, 0, pl.ds(w_start, wk), :]
    s_w = per_head(_nt_dot(q4, kw))
    wpos = w_start + lax.broadcasted_iota(jnp.int32, (qb, wk), 1)
    dist = t_row - wpos
    mask_w = (dist >= 0) & (dist < WINDOW)
    o_win = [_dot(_masked_softmax_rows(s, mask_w).astype(BF16), vw) for s in s_w]

    gates = gate_ref[0, 0]
    out = jnp.zeros((qb, width), F32)
    for r in range(rep):
        o_r = (gates[:, 3 * r:3 * r + 1] * o_cmp[r] + gates[:, 3 * r + 1:3 * r + 2] * o_slc[r]
               + gates[:, 3 * r + 2:3 * r + 3] * o_win[r])
        out = jnp.where(lane_head == r, o_r, out)
    o_ref[...] = out.astype(o_ref.dtype)


def _nsa_attention(kvq, gates, kv_cmp, ks, vs, kw, vw, batch, seq):
    t = kvq.shape[0]
    g, rep, dh = NSA_GROUPS, NSA_REP, NSA_DH
    width = rep * dh
    nq = seq // Q_BLOCK
    n_cmp_pad = kv_cmp.shape[3]
    nsel = seq // SLC_BLOCK
    nsel_pad = max(LANES, nsel)
    kc_len = min(SLC_CHUNK, seq)
    ci = np.arange(n_cmp_pad)[None, :]
    sj = np.arange(nsel_pad)[:, None]
    overlap_t = ((ci * CMP_STRIDE < (sj + 1) * SLC_BLOCK) & (ci * CMP_STRIDE + CMP_LEN > sj * SLC_BLOCK)
                 & (ci < n_cmp_pad - 1) & (sj < nsel))
    overlap_t = jnp.asarray(overlap_t, BF16)
    key_blk = np.arange(seq).reshape(seq // kc_len, 1, kc_len) // SLC_BLOCK
    expand = jnp.asarray(key_blk == np.arange(nsel_pad)[None, :, None], BF16)
    q_off = 6 * KV_W // width
    kern = functools.partial(_nsa_attn_kernel, seq=seq, top_k=min(SLC_TOPK, nsel))
    per_bg = lambda b, gg, i: (b, gg, 0, 0)
    return pl.pallas_call(
        kern,
        grid=(batch, g, nq),
        in_specs=[
            pl.BlockSpec((Q_BLOCK, width), lambda b, gg, i: (b * nq + i, q_off + gg)),
            pl.BlockSpec((1, 1, Q_BLOCK, 3 * rep), lambda b, gg, i: (b, gg, i, 0)),
            pl.BlockSpec((1, 1, n_cmp_pad, width), per_bg),
            pl.BlockSpec((1, 1, n_cmp_pad, width), per_bg),
            pl.BlockSpec((1, 1, seq, width), per_bg),
            pl.BlockSpec((1, 1, seq, width), per_bg),
            pl.BlockSpec((1, 1, seq, width), per_bg),
            pl.BlockSpec((1, 1, seq, width), per_bg),
            pl.BlockSpec((nsel_pad, n_cmp_pad), lambda b, gg, i: (0, 0)),
            pl.BlockSpec((seq // kc_len, nsel_pad, kc_len), lambda b, gg, i: (0, 0, 0)),
        ],
        out_specs=pl.BlockSpec((Q_BLOCK, width), lambda b, gg, i: (b * nq + i, gg)),
        out_shape=jax.ShapeDtypeStruct((t, NSA_HEADS * dh), BF16),
        compiler_params=_cparams(("parallel", "parallel", "arbitrary")),
        name="nsa_attention",
    )(kvq, gates, kv_cmp[0], kv_cmp[1], ks, vs, kw, vw, overlap_t, expand)


def _moe_layer(h1, ridx, rw, wg, wu, wd, ln_g, ln_b):
    tile_expert, row_token, n_used, pos0, pos1, w0, w1 = _moe_plan(ridx, rw)
    ys = _experts(h1, tile_expert, row_token, n_used, wg, wu, wd)
    return _combine(ys, h1, pos0, pos1, w0, w1, ln_g, ln_b)


def kernel(x, ret_w_in, ret_w_out, nsa_w_kv, cmp_pe_k, cmp_pe_v, cmp_k_w1, cmp_k_w2, cmp_v_w1, cmp_v_w2,
           nsa_w_in, nsa_w_out, router_w, router_b, moe_w_gate, moe_w_up, moe_w_down,
           ln_mix_g, ln_mix_b, ln_ffn_g, ln_ffn_b):
    batch, seq, d = x.shape
    t = batch * seq
    h = x.reshape(t, d)

    rwt = router_w.T
    rwt_hi = rwt.astype(BF16)
    rwt_lo = (rwt - rwt_hi.astype(F32)).astype(BF16)
    rb = router_b.reshape(N_EXPERTS, 1).astype(F32)
    row = lambda v: v.reshape(1, d)

    proj = _ret_proj(h, ret_w_in[0].astype(BF16), seq)
    gated = _ret_core(proj, batch, seq)
    h1, ridx, rw = _mix_out(gated, ret_w_out[0].astype(BF16), h, row(ln_mix_g[0]), row(ln_mix_b[0]),
                            rwt_hi, rwt_lo, rb)
    h = _moe_layer(h1, ridx, rw, moe_w_gate[0].astype(BF16), moe_w_up[0].astype(BF16),
                   moe_w_down[0].astype(BF16), row(ln_ffn_g[0]), row(ln_ffn_b[0]))

    g, dh, rep = NSA_GROUPS, NSA_DH, NSA_REP
    n_q = NSA_HEADS * dh
    w_all = jnp.concatenate([nsa_w_kv, nsa_w_in[0][:, :n_q]], axis=1).astype(BF16)
    kvq = _nsa_proj(h, w_all, seq)
    w_gate = jnp.pad(nsa_w_in[0][:, n_q:], ((0, 0), (0, LANES - 3 * NSA_HEADS))).astype(BF16)
    gates = _nsa_gates(h, w_gate)[:, :3 * NSA_HEADS]
    gates = gates.reshape(batch, seq, g, 3 * rep).transpose(0, 2, 1, 3)

    parts = kvq[:, :6 * KV_W].reshape(batch, seq, 6, g, dh)

    def blocks16(p):
        return p.reshape(batch, seq // CMP_STRIDE, CMP_STRIDE, g, dh).transpose(0, 3, 1, 2, 4).reshape(
            batch, g, seq // CMP_STRIDE, CMP_STRIDE * dh)

    def lanes4(p):
        return jnp.tile(p.transpose(0, 2, 1, 3), (1, 1, 1, rep))

    xkv = jnp.stack([blocks16(parts[:, :, 0]), blocks16(parts[:, :, 1])])
    half_w = CMP_STRIDE * dh
    pe = jnp.stack([cmp_pe_k.reshape(2, half_w), cmp_pe_v.reshape(2, half_w)]).astype(F32)
    w1 = jnp.stack([cmp_k_w1.reshape(2, half_w, CMP_HIDDEN), cmp_v_w1.reshape(2, half_w, CMP_HIDDEN)]).astype(BF16)
    w2 = jnp.stack([cmp_k_w2, cmp_v_w2]).astype(BF16)
    kv_cmp = jnp.tile(_compress(xkv, pe, w1, w2), (1, 1, 1, 1, rep))

    attn = _nsa_attention(kvq, gates, kv_cmp, lanes4(parts[:, :, 2]), lanes4(parts[:, :, 3]),
                          lanes4(parts[:, :, 4]), lanes4(parts[:, :, 5]), batch, seq)
    h1, ridx, rw = _mix_out(attn, nsa_w_out[0].astype(BF16), h, row(ln_mix_g[1]), row(ln_mix_b[1]),
                            rwt_hi, rwt_lo, rb)
    h = _moe_layer(h1, ridx, rw, moe_w_gate[1].astype(BF16), moe_w_up[1].astype(BF16),
                   moe_w_down[1].astype(BF16), row(ln_ffn_g[1]), row(ln_ffn_b[1]))
    return h.reshape(batch, seq, d)
```

```python
import functools
import math

import jax
import jax.numpy as jnp
import numpy as np
from jax import lax
from jax.experimental import pallas as pl
from jax.experimental.pallas import tpu as pltpu

F32 = jnp.float32
BF16 = jnp.bfloat16

D_MODEL = 1024
DEPTH = 2
ALPHA = (2.0 * DEPTH) ** 0.25
LN_EPS = 1e-5
ROPE_THETA = 10000.0
NEG = -1e30

RET_HEADS = 4
RET_DK = D_MODEL // RET_HEADS
RET_DV = 2 * RET_DK
RET_CHUNK = 256

NSA_HEADS = 16
NSA_GROUPS = 4
NSA_REP = NSA_HEADS // NSA_GROUPS
NSA_DH = D_MODEL // NSA_HEADS
KV_W = NSA_GROUPS * NSA_DH
CMP_LEN = 32
CMP_STRIDE = 16
CMP_HIDDEN = 256
SLC_BLOCK = 64
SLC_TOPK = 16
WINDOW = 512
Q_BLOCK = 128
FORCE_BONUS = 1e3
SLC_CHUNK = 512
SLC_SUBCHUNK = 256

N_EXPERTS = 16
N_EXPERT_GROUPS = 4
EXPERTS_PER_GROUP = N_EXPERTS // N_EXPERT_GROUPS
D_EXPERT = 512

LANES = 128
HEAD_PITCH = LANES
ROW_TILE = 512
MOE_TILE = 256
VMEM_LIMIT = 56 * 1024 * 1024


def _cparams(sem):
    return pltpu.CompilerParams(dimension_semantics=sem, vmem_limit_bytes=VMEM_LIMIT)


def _nt_dot(a, b):
    return lax.dot_general(a, b, (((1,), (1,)), ((), ())), preferred_element_type=F32)


def _dot(a, b):
    return jnp.dot(a, b, preferred_element_type=F32)


def _rope_tables_half(seq, d):
    inv = ROPE_THETA ** (-jnp.arange(0, d, 2, dtype=F32) / d)
    ang = jnp.arange(seq, dtype=F32)[:, None] * inv[None, :]
    return jnp.cos(ang), jnp.sin(ang)


def _rope_tables_pitched(seq, d, pitch, width):
    cos, sin = _rope_tables_half(seq, d)
    zero = jnp.zeros_like(sin)
    pad = jnp.zeros((seq, pitch - d), F32)
    reps = width // pitch
    cos_t = jnp.tile(jnp.concatenate([cos, cos, pad], -1), (1, reps))
    sin_lo = jnp.tile(jnp.concatenate([-sin, zero, pad], -1), (1, reps))
    sin_hi = jnp.tile(jnp.concatenate([zero, sin, pad], -1), (1, reps))
    return cos_t, sin_lo, sin_hi


def _ret_proj_kernel(x_ref, w_ref, cos_ref, sin_ref, o_ref, *, n_rope_tiles, k_scale):
    j = pl.program_id(1)
    acc = _dot(x_ref[...].astype(BF16), w_ref[...])
    tn = acc.shape[1]
    half = RET_DK // 2

    @pl.when(j < n_rope_tiles)
    def _():
        c = cos_ref[...]
        s = sin_ref[...]
        scale = jnp.where(j >= n_rope_tiles // 2, k_scale, 1.0).astype(F32)
        for hh in range(tn // RET_DK):
            lo = hh * RET_DK
            x1 = acc[:, lo:lo + half]
            x2 = acc[:, lo + half:lo + RET_DK]
            o_ref[:, lo:lo + half] = ((x1 * c - x2 * s) * scale).astype(o_ref.dtype)
            o_ref[:, lo + half:lo + RET_DK] = ((x2 * c + x1 * s) * scale).astype(o_ref.dtype)

    @pl.when(j >= n_rope_tiles)
    def _():
        o_ref[...] = acc.astype(o_ref.dtype)


def _ret_proj(x2d, w_bf16, seq):
    t, d = x2d.shape
    n = w_bf16.shape[1]
    tm, tn = ROW_TILE, 1024
    cos, sin = _rope_tables_half(seq, RET_DK)
    tiles_per_seq = seq // tm
    n_rope_tiles = 2 * RET_HEADS * RET_DK // tn
    kern = functools.partial(_ret_proj_kernel, n_rope_tiles=n_rope_tiles, k_scale=RET_DK ** -0.5)
    return pl.pallas_call(
        kern,
        grid=(t // tm, n // tn),
        in_specs=[
            pl.BlockSpec((tm, d), lambda i, j: (i, 0)),
            pl.BlockSpec((d, tn), lambda i, j: (0, j)),
            pl.BlockSpec((tm, RET_DK // 2), lambda i, j: (i % tiles_per_seq, 0)),
            pl.BlockSpec((tm, RET_DK // 2), lambda i, j: (i % tiles_per_seq, 0)),
        ],
        out_specs=pl.BlockSpec((tm, tn), lambda i, j: (i, j)),
        out_shape=jax.ShapeDtypeStruct((t, n), BF16),
        compiler_params=_cparams(("parallel", "arbitrary")),
        name="ret_proj",
    )(x2d, w_bf16, cos, sin)


def _ret_core_kernel(q_ref, k_ref, v_ref, g_ref, mask_ref, qd_ref, kd_ref, cd_ref, o_ref, state_ref):
    c = pl.program_id(2)

    @pl.when(c == 0)
    def _():
        state_ref[...] = jnp.zeros_like(state_ref)

    q = q_ref[...]
    k = k_ref[...]
    v = v_ref[...]
    inner = _nt_dot(q, k) * mask_ref[0]
    state = state_ref[...]
    qs = (q.astype(F32) * qd_ref[0]).astype(BF16)
    o = _dot(inner.astype(BF16), v) + _dot(qs, state.astype(BF16))
    ks = k.astype(F32) * kd_ref[0]
    state_ref[...] = state * cd_ref[0] + _dot(ks.T.astype(BF16), v)

    mu = jnp.mean(o, axis=-1, keepdims=True)
    var = jnp.mean(jnp.square(o - mu), axis=-1, keepdims=True)
    on = (o - mu) * lax.rsqrt(var + LN_EPS)
    g = g_ref[...].astype(F32)
    o_ref[...] = (g * jax.nn.sigmoid(g) * on).astype(o_ref.dtype)


def _ret_core(proj, batch, seq):
    t = proj.shape[0]
    h, dk, dv, c = RET_HEADS, RET_DK, RET_DV, min(RET_CHUNK, seq)
    nc = seq // c
    log_gamma = jnp.log1p(-(2.0 ** (-5.0 - jnp.arange(h, dtype=F32))))
    idx = jnp.arange(c, dtype=F32)
    diff = idx[:, None] - idx[None, :]
    mask = jnp.where(diff >= 0, jnp.exp(log_gamma[:, None, None] * jnp.maximum(diff, 0.0)), 0.0)
    qd = jnp.broadcast_to(jnp.exp(log_gamma[:, None] * (idx[None, :] + 1.0))[:, :, None], (h, c, dk))
    kd = jnp.broadcast_to(jnp.exp(log_gamma[:, None] * (c - 1.0 - idx[None, :]))[:, :, None], (h, c, dk))
    cd = jnp.broadcast_to(jnp.exp(log_gamma * c)[:, None, None], (h, 1, dv))
    v_off = 2 * h * dk // dv
    g_off = v_off + h
    return pl.pallas_call(
        _ret_core_kernel,
        grid=(batch, h, nc),
        in_specs=[
            pl.BlockSpec((c, dk), lambda b, hh, cc: (b * nc + cc, hh)),
            pl.BlockSpec((c, dk), lambda b, hh, cc: (b * nc + cc, h + hh)),
            pl.BlockSpec((c, dv), lambda b, hh, cc: (b * nc + cc, v_off + hh)),
            pl.BlockSpec((c, dv), lambda b, hh, cc: (b * nc + cc, g_off + hh)),
            pl.BlockSpec((1, c, c), lambda b, hh, cc: (hh, 0, 0)),
            pl.BlockSpec((1, c, dk), lambda b, hh, cc: (hh, 0, 0)),
            pl.BlockSpec((1, c, dk), lambda b, hh, cc: (hh, 0, 0)),
            pl.BlockSpec((1, 1, dv), lambda b, hh, cc: (hh, 0, 0)),
        ],
        out_specs=pl.BlockSpec((c, dv), lambda b, hh, cc: (b * nc + cc, hh)),
        out_shape=jax.ShapeDtypeStruct((t, h * dv), BF16),
        scratch_shapes=[pltpu.VMEM((dk, dv), F32)],
        compiler_params=_cparams(("parallel", "parallel", "arbitrary")),
        name="ret_core",
    )(proj, proj, proj, proj, mask, qd, kd, cd)


def _layer_norm_rows(y, g, b):
    mu = jnp.mean(y, axis=-1, keepdims=True)
    var = jnp.mean(jnp.square(y - mu), axis=-1, keepdims=True)
    return (y - mu) * lax.rsqrt(var + LN_EPS) * g + b


def _split_bf16(x):
    hi = x.astype(BF16)
    lo = (x - hi.astype(F32)).astype(BF16)
    return hi, lo


def _route(hn, rwt_hi_ref, rwt_lo_ref, rb_ref, ridx_ref, rw_ref):
    h_hi, h_lo = _split_bf16(hn)
    w_hi = rwt_hi_ref[...]
    logits = _nt_dot(w_hi, h_hi) + _nt_dot(w_hi, h_lo) + _nt_dot(rwt_lo_ref[...], h_hi)
    aff = jax.nn.sigmoid(logits)
    sel = aff + rb_ref[...]
    tm = hn.shape[0]

    def row(a, e):
        return a[e:e + 1, :]

    best_g = jnp.zeros((1, tm), jnp.int32)
    best_s = None
    for gi in range(N_EXPERT_GROUPS):
        a0, a1, a2, a3 = [row(sel, EXPERTS_PER_GROUP * gi + e) for e in range(EXPERTS_PER_GROUP)]
        hi01, lo01 = jnp.maximum(a0, a1), jnp.minimum(a0, a1)
        hi23, lo23 = jnp.maximum(a2, a3), jnp.minimum(a2, a3)
        top2 = jnp.maximum(hi01, hi23) + jnp.maximum(jnp.minimum(hi01, hi23), jnp.maximum(lo01, lo23))
        if best_s is None:
            best_s = top2
        else:
            upd = top2 > best_s
            best_g = jnp.where(upd, gi, best_g)
            best_s = jnp.where(upd, top2, best_s)

    def pick(a, e):
        out = row(a, e)
        for gi in range(1, N_EXPERT_GROUPS):
            out = jnp.where(best_g == gi, row(a, EXPERTS_PER_GROUP * gi + e), out)
        return out

    cs = [pick(sel, e) for e in range(EXPERTS_PER_GROUP)]
    af = [pick(aff, e) for e in range(EXPERTS_PER_GROUP)]
    i1 = jnp.zeros((1, tm), jnp.int32)
    s1, a1v = cs[0], af[0]
    for e in range(1, EXPERTS_PER_GROUP):
        upd = cs[e] > s1
        i1 = jnp.where(upd, e, i1)
        s1 = jnp.where(upd, cs[e], s1)
        a1v = jnp.where(upd, af[e], a1v)
    i2 = jnp.full((1, tm), -1, jnp.int32)
    s2 = jnp.full((1, tm), -jnp.inf, F32)
    a2v = jnp.zeros((1, tm), F32)
    for e in range(EXPERTS_PER_GROUP):
        upd = (i1 != e) & ((i2 < 0) | (cs[e] > s2))
        i2 = jnp.where(upd, e, i2)
        s2 = jnp.where(upd, cs[e], s2)
        a2v = jnp.where(upd, af[e], a2v)
    tot = a1v + a2v
    base = best_g * EXPERTS_PER_GROUP
    zi = jnp.zeros((6, tm), jnp.int32)
    zf = jnp.zeros((6, tm), F32)
    ridx_ref[...] = jnp.concatenate([base + i1, base + i2, zi], axis=0)
    rw_ref[...] = jnp.concatenate([a1v / tot, a2v / tot, zf], axis=0)


def _mix_out_kernel(a_ref, w_ref, h_ref, g_ref, b_ref, rwt_hi_ref, rwt_lo_ref, rb_ref,
                    o_ref, ridx_ref, rw_ref):
    mix = _dot(a_ref[...], w_ref[...])
    hn = _layer_norm_rows(ALPHA * h_ref[...] + mix, g_ref[...], b_ref[...])
    o_ref[...] = hn
    _route(hn, rwt_hi_ref, rwt_lo_ref, rb_ref, ridx_ref, rw_ref)


def _mix_out(a_bf16, w_bf16, h2d, ln_g, ln_b, rwt_hi, rwt_lo, rb):
    t, d = h2d.shape
    k = a_bf16.shape[1]
    tm = ROW_TILE
    row = lambda i: (i, 0)
    fixed = lambda i: (0, 0)
    return pl.pallas_call(
        _mix_out_kernel,
        grid=(t // tm,),
        in_specs=[
            pl.BlockSpec((tm, k), row),
            pl.BlockSpec((k, d), fixed),
            pl.BlockSpec((tm, d), row),
            pl.BlockSpec((1, d), fixed),
            pl.BlockSpec((1, d), fixed),
            pl.BlockSpec((N_EXPERTS, d), fixed),
            pl.BlockSpec((N_EXPERTS, d), fixed),
            pl.BlockSpec((N_EXPERTS, 1), fixed),
        ],
        out_specs=[
            pl.BlockSpec((tm, d), row),
            pl.BlockSpec((8, tm), lambda i: (0, i)),
            pl.BlockSpec((8, tm), lambda i: (0, i)),
        ],
        out_shape=[
            jax.ShapeDtypeStruct((t, d), F32),
            jax.ShapeDtypeStruct((8, t), jnp.int32),
            jax.ShapeDtypeStruct((8, t), F32),
        ],
        compiler_params=_cparams(("parallel",)),
        name="mix_out",
    )(a_bf16, w_bf16, h2d, ln_g, ln_b, rwt_hi, rwt_lo, rb)


def _expert_kernel(te_ref, rt_ref, nused_ref, h_hbm, wg_ref, wu_ref, wd_ref, o_ref, xbuf, sem):
    i = pl.program_id(0)
    tile = xbuf.shape[0]

    def row_copy(r):
        tok = rt_ref[i * tile + r]
        return pltpu.make_async_copy(h_hbm.at[pl.ds(tok, 1)], xbuf.at[pl.ds(r, 1)], sem)

    @pl.when(i < nused_ref[0])
    def _():
        def start(r, carry):
            row_copy(r).start()
            return carry

        def wait(r, carry):
            row_copy(r).wait()
            return carry

        lax.fori_loop(0, tile, start, 0)
        lax.fori_loop(0, tile, wait, 0)
        x = xbuf[...].astype(BF16)
        gate = _dot(x, wg_ref[0])
        up = _dot(x, wu_ref[0])
        act = (gate * jax.nn.sigmoid(gate) * up).astype(BF16)
        o_ref[...] = _dot(act, wd_ref[0])

    @pl.when(i >= nused_ref[0])
    def _():
        o_ref[...] = jnp.zeros_like(o_ref)


def _experts(h1, tile_expert, row_token, n_used, wg, wu, wd):
    t, d = h1.shape
    p = row_token.shape[0]
    n_tiles = p // MOE_TILE
    de = wg.shape[2]
    grid_spec = pltpu.PrefetchScalarGridSpec(
        num_scalar_prefetch=3,
        grid=(n_tiles,),
        in_specs=[
            pl.BlockSpec(memory_space=pl.ANY),
            pl.BlockSpec((1, d, de), lambda i, te, rt, nu: (te[i], 0, 0)),
            pl.BlockSpec((1, d, de), lambda i, te, rt, nu: (te[i], 0, 0)),
            pl.BlockSpec((1, de, d), lambda i, te, rt, nu: (te[i], 0, 0)),
        ],
        out_specs=pl.BlockSpec((MOE_TILE, d), lambda i, te, rt, nu: (i, 0)),
        scratch_shapes=[pltpu.VMEM((MOE_TILE, d), F32), pltpu.SemaphoreType.DMA(())],
    )
    return pl.pallas_call(
        _expert_kernel,
        grid_spec=grid_spec,
        out_shape=jax.ShapeDtypeStruct((p, d), F32),
        compiler_params=_cparams(("arbitrary",)),
        name="moe_experts",
    )(tile_expert, row_token, n_used, h1, wg, wu, wd)


def _combine_kernel(p0_ref, p1_ref, ys_hbm, h_ref, w0_ref, w1_ref, g_ref, b_ref, o_ref, buf0, buf1, sem):
    i = pl.program_id(0)
    tm = buf0.shape[0]

    def copies(r):
        c0 = pltpu.make_async_copy(ys_hbm.at[pl.ds(p0_ref[i * tm + r], 1)], buf0.at[pl.ds(r, 1)], sem.at[0])
        c1 = pltpu.make_async_copy(ys_hbm.at[pl.ds(p1_ref[i * tm + r], 1)], buf1.at[pl.ds(r, 1)], sem.at[1])
        return c0, c1

    def start(r, carry):
        c0, c1 = copies(r)
        c0.start()
        c1.start()
        return carry

    def wait(r, carry):
        c0, c1 = copies(r)
        c0.wait()
        c1.wait()
        return carry

    lax.fori_loop(0, tm, start, 0)
    lax.fori_loop(0, tm, wait, 0)
    y = ALPHA * h_ref[...] + (w0_ref[...] * buf0[...] + w1_ref[...] * buf1[...])
    o_ref[...] = _layer_norm_rows(y, g_ref[...], b_ref[...])


def _combine(ys, h1, pos0, pos1, w0, w1, ln_g, ln_b):
    t, d = h1.shape
    tm = MOE_TILE
    row = lambda i, p0, p1: (i, 0)
    fixed = lambda i, p0, p1: (0, 0)
    grid_spec = pltpu.PrefetchScalarGridSpec(
        num_scalar_prefetch=2,
        grid=(t // tm,),
        in_specs=[
            pl.BlockSpec(memory_space=pl.ANY),
            pl.BlockSpec((tm, d), row),
            pl.BlockSpec((tm, 1), row),
            pl.BlockSpec((tm, 1), row),
            pl.BlockSpec((1, d), fixed),
            pl.BlockSpec((1, d), fixed),
        ],
        out_specs=pl.BlockSpec((tm, d), row),
        scratch_shapes=[pltpu.VMEM((tm, d), F32), pltpu.VMEM((tm, d), F32), pltpu.SemaphoreType.DMA((2,))],
    )
    return pl.pallas_call(
        _combine_kernel,
        grid_spec=grid_spec,
        out_shape=jax.ShapeDtypeStruct((t, d), F32),
        compiler_params=_cparams(("arbitrary",)),
        name="moe_combine",
    )(pos0, pos1, ys, h1, w0, w1, ln_g, ln_b)


def _moe_plan(ridx, rw):
    t = ridx.shape[1]
    e = jnp.concatenate([ridx[0], ridx[1]])
    onehot = (e[:, None] == jnp.arange(N_EXPERTS)[None, :]).astype(jnp.int32)
    csum = jnp.cumsum(onehot, axis=0)
    rank = jnp.take_along_axis(csum, e[:, None], axis=1)[:, 0] - 1
    counts = csum[-1]
    padded = (counts + MOE_TILE - 1) // MOE_TILE * MOE_TILE
    ends = jnp.cumsum(padded)
    pos = (ends - padded)[e] + rank
    p = 2 * t + N_EXPERTS * MOE_TILE
    tok = jnp.concatenate([jnp.arange(t, dtype=jnp.int32)] * 2)
    row_token = jnp.zeros((p,), jnp.int32).at[pos].set(tok)
    tile_start = jnp.arange(p // MOE_TILE, dtype=jnp.int32) * MOE_TILE
    tile_expert = jnp.sum((tile_start[:, None] >= ends[None, :]).astype(jnp.int32), axis=1)
    tile_expert = jnp.minimum(tile_expert, N_EXPERTS - 1)
    n_used = (ends[-1] // MOE_TILE).astype(jnp.int32).reshape(1)
    return (tile_expert, row_token, n_used, pos[:t].astype(jnp.int32), pos[t:].astype(jnp.int32),
            rw[0][:, None], rw[1][:, None])


def _nsa_proj_kernel(x_ref, w_ref, cos_ref, slo_ref, shi_ref, o_ref, *, n_kv_tiles, q_scale):
    j = pl.program_id(1)
    acc = _dot(x_ref[...].astype(BF16), w_ref[...])
    tn = acc.shape[1]
    half = NSA_DH // 2
    is_q = j >= n_kv_tiles
    roped = is_q | (j % 2 == 0)

    @pl.when(roped)
    def _():
        rot = (pltpu.roll(acc, tn - half, axis=1) * slo_ref[...] + pltpu.roll(acc, half, axis=1) * shi_ref[...])
        scale = jnp.where(is_q, q_scale, 1.0).astype(F32)
        o_ref[...] = ((acc * cos_ref[...] + rot) * scale).astype(o_ref.dtype)

    @pl.when(jnp.logical_not(roped))
    def _():
        lane = lax.broadcasted_iota(jnp.int32, (1, tn), 1)
        o_ref[...] = (acc + jnp.where(lane % HEAD_PITCH == NSA_DH, 1.0, 0.0)).astype(o_ref.dtype)


def _nsa_proj(x2d, w_bf16, seq):
    t, d = x2d.shape
    n = w_bf16.shape[1]
    tm, tn = ROW_TILE, NSA_GROUPS * HEAD_PITCH
    cos_t, sin_lo, sin_hi = _rope_tables_pitched(seq, NSA_DH, HEAD_PITCH, tn)
    tiles_per_seq = seq // tm
    tab = lambda i, j: (i % tiles_per_seq, 0)
    kern = functools.partial(_nsa_proj_kernel, n_kv_tiles=6, q_scale=NSA_DH ** -0.5 * math.log2(math.e))
    return pl.pallas_call(
        kern,
        grid=(t // tm, n // tn),
        in_specs=[
            pl.BlockSpec((tm, d), lambda i, j: (i, 0)),
            pl.BlockSpec((d, tn), lambda i, j: (0, j)),
            pl.BlockSpec((tm, tn), tab),
            pl.BlockSpec((tm, tn), tab),
            pl.BlockSpec((tm, tn), tab),
        ],
        out_specs=pl.BlockSpec((tm, tn), lambda i, j: (i, j)),
        out_shape=jax.ShapeDtypeStruct((t, n), BF16),
        compiler_params=_cparams(("parallel", "arbitrary")),
        name="nsa_proj",
    )(x2d, w_bf16, cos_t, sin_lo, sin_hi)


def _gate_kernel(x_ref, w_ref, o_ref):
    o_ref[...] = jax.nn.sigmoid(_dot(x_ref[...].astype(BF16), w_ref[...]))


def _nsa_gates(x2d, wg_bf16):
    t, d = x2d.shape
    tm = ROW_TILE
    return pl.pallas_call(
        _gate_kernel,
        grid=(t // tm,),
        in_specs=[pl.BlockSpec((tm, d), lambda i: (i, 0)), pl.BlockSpec((d, LANES), lambda i: (0, 0))],
        out_specs=pl.BlockSpec((tm, LANES), lambda i: (i, 0)),
        out_shape=jax.ShapeDtypeStruct((t, LANES), F32),
        compiler_params=_cparams(("parallel",)),
        name="nsa_gates",
    )(x2d, wg_bf16)


def _gelu_tanh(x):
    return 0.5 * x * (1.0 + jnp.tanh(math.sqrt(2.0 / math.pi) * (x + 0.044715 * (x * x * x))))


def _cmp_kernel(x_ref, pe_ref, w1_ref, w2_ref, o_ref):
    x = x_ref[0, 0, 0].astype(F32)
    n = x.shape[0]
    xa = (x + pe_ref[0, 0:1, :]).astype(BF16)
    xb = (x + pe_ref[0, 1:2, :]).astype(BF16)
    u = _dot(xa, w1_ref[0, 0])
    v = _dot(xb, w1_ref[0, 1])
    pre = u + pltpu.roll(v, n - 1, axis=0)
    out = _dot(_gelu_tanh(pre).astype(BF16), w2_ref[0])
    lane = lax.broadcasted_iota(jnp.int32, (1, out.shape[1]), 1)
    ones_lane = jnp.where((lane == NSA_DH) & (pl.program_id(0) == 1), 1.0, 0.0)
    o_ref[0, 0, 0] = (out + ones_lane).astype(o_ref.dtype)


def _compress(xkv, pe, w1, w2):
    _, b, g, n, w = xkv.shape
    pitch = w2.shape[2]
    return pl.pallas_call(
        _cmp_kernel,
        grid=(2, b, g),
        in_specs=[
            pl.BlockSpec((1, 1, 1, n, w), lambda s, bb, gg: (s, bb, gg, 0, 0)),
            pl.BlockSpec((1, 2, w), lambda s, bb, gg: (s, 0, 0)),
            pl.BlockSpec((1, 2, w, CMP_HIDDEN), lambda s, bb, gg: (s, 0, 0, 0)),
            pl.BlockSpec((1, CMP_HIDDEN, pitch), lambda s, bb, gg: (s, 0, 0)),
        ],
        out_specs=pl.BlockSpec((1, 1, 1, n, pitch), lambda s, bb, gg: (s, bb, gg, 0, 0)),
        out_shape=jax.ShapeDtypeStruct((2, b, g, n, pitch), BF16),
        compiler_params=_cparams(("parallel", "parallel", "parallel")),
        name="nsa_compress",
    )(xkv, pe, w1, w2)


def _softmax_pv(s_t, v_t):
    m = jnp.max(s_t, axis=0, keepdims=True)
    p = jnp.exp2(s_t - m).astype(BF16)
    return p, _dot(v_t, p)


def _normalize(ov_t):
    return ov_t * (1.0 / ov_t[NSA_DH:NSA_DH + 1, :])


def _nsa_attn_kernel(q_ref, gate_ref, kc_ref, vc_ref, ks_ref, vs_ref, kw_ref, vw_ref, ovl_ref, xt_ref,
                     o_ref, m_ref, acc_ref, s_ref, *, seq, top_k):
    i = pl.program_id(2)
    qb, rep, pitch = Q_BLOCK, NSA_REP, HEAD_PITCH
    t0 = i * qb
    q2 = q_ref[...]
    q4 = jnp.concatenate([q2[:, r * pitch:(r + 1) * pitch] for r in range(rep)], axis=0)
    t_lane = t0 + lax.broadcasted_iota(jnp.int32, (1, qb), 1)
    t_lane4 = jnp.concatenate([t_lane] * rep, axis=1)

    def add_cols(s_t, bias_t):
        return jnp.concatenate([s_t[:, r * qb:(r + 1) * qb] + bias_t for r in range(rep)], axis=1)

    n_cmp_pad = kc_ref.shape[2]
    cmp_end = lax.broadcasted_iota(jnp.int32, (n_cmp_pad, qb), 0) * CMP_STRIDE + (CMP_LEN - 1)
    s_c = add_cols(_nt_dot(kc_ref[0, 0], q4), jnp.where(cmp_end <= t_lane, 0.0, NEG))
    p_c, ov_c = _softmax_pv(s_c, vc_ref[0, 0])
    o_cmp = jnp.where(t_lane4 >= CMP_LEN - 1, _normalize(ov_c), 0.0)

    ovl = ovl_ref[...]
    nsel = ovl.shape[0] - 8
    imp = jnp.zeros((nsel, qb), F32)
    for r in range(rep):
        ext = _dot(ovl, p_c[:, r * qb:(r + 1) * qb])
        imp = imp + ext[:nsel] * (1.0 / ext[nsel:nsel + 1])

    def values_t(v_ref, first_tile, n_tiles):
        return jnp.concatenate([v_ref[0, 0, first_tile + u] for u in range(n_tiles)], axis=1)

    wk = min(seq, qb + WINDOW)
    w_start = pl.multiple_of(jnp.maximum(t0 + qb - wk, 0), qb)
    dist = t_lane - (w_start + lax.broadcasted_iota(jnp.int32, (wk, qb), 0))
    bias_w = jnp.where((dist >= 0) & (dist < WINDOW), 0.0, NEG)
    s_w = add_cols(_nt_dot(kw_ref[pl.ds(w_start, wk), :], q4), bias_w)
    _, ov_w = _softmax_pv(s_w, values_t(vw_ref, w_start // LANES, wk // LANES))
    o_win = _normalize(ov_w)

    blk = lax.broadcasted_iota(jnp.int32, (nsel, qb), 0)
    t_col = t0 + lax.broadcasted_iota(jnp.int32, (nsel, qb), 1)
    cur = t_col // SLC_BLOCK
    valid = (blk * SLC_BLOCK <= t_col) & (blk < seq // SLC_BLOCK)
    forced = (blk == 0) | (blk == cur) | (blk == cur - 1)
    score = jnp.where(valid, imp + jnp.where(forced, FORCE_BONUS, 0.0), NEG)
    blk_f = blk.astype(F32)
    chosen = jnp.zeros((nsel, qb), F32)
    for _ in range(top_k):
        m = jnp.max(score, axis=0, keepdims=True)
        first = jnp.min(jnp.where(score == m, blk_f, float(nsel)), axis=0, keepdims=True)
        hit = blk_f == first
        chosen = jnp.where(hit, 1.0, chosen)
        score = jnp.where(hit, -jnp.inf, score)
    chosen = jnp.where(valid, chosen, 0.0)
    sel_bias = ((chosen.T - 1.0) * -NEG).astype(BF16)

    q_aug = jnp.concatenate([q4, jnp.concatenate([sel_bias] * rep, axis=0)], axis=1)
    kc_len = min(SLC_CHUNK, seq)
    last = t0 // kc_len
    k_row = lax.broadcasted_iota(jnp.int32, (kc_len, qb), 0)
    q_lane = lax.broadcasted_iota(jnp.int32, (kc_len, qb), 1) + (t0 - last * kc_len)
    causal = jnp.where(k_row <= q_lane, 0.0, NEG)

    def slc_scores(c, slot):
        start = pl.multiple_of(c * kc_len, kc_len)
        k_aug = jnp.concatenate([ks_ref[pl.ds(start, kc_len), :], xt_ref[pl.ds(start, kc_len), :]], axis=1)
        s = _nt_dot(k_aug, q_aug)
        s_ref[slot] = add_cols(s, causal * (c == last).astype(F32))

    def slc_update(c, slot, m_old, acc_old):
        s = s_ref[slot]
        m_new = jnp.maximum(m_old, jnp.max(s, axis=0, keepdims=True))
        p = jnp.exp2(s - m_new).astype(BF16)
        v_t = values_t(vs_ref, c * (kc_len // LANES), kc_len // LANES)
        return m_new, jnp.exp2(m_old - m_new) * acc_old + _dot(v_t, p)

    def chunk_pair(j, carry):
        slc_scores(2 * j + 1, 1)
        m, acc = slc_update(2 * j, 0, m_ref[...], acc_ref[...])
        slc_scores(jnp.minimum(2 * j + 2, last), 0)
        m, acc = slc_update(2 * j + 1, 1, m, acc)
        m_ref[...] = m
        acc_ref[...] = acc
        return carry

    m_ref[...] = jnp.full(m_ref.shape, NEG, F32)
    acc_ref[...] = jnp.zeros_like(acc_ref)
    slc_scores(0, 0)
    lax.fori_loop(0, (last + 1) // 2, chunk_pair, 0)

    @pl.when(last % 2 == 0)
    def _():
        m, acc = slc_update(last, 0, m_ref[...], acc_ref[...])
        m_ref[...] = m
        acc_ref[...] = acc

    o_slc = _normalize(acc_ref[...])

    gates = gate_ref[0, 0]
    head_lane = lax.broadcasted_iota(jnp.int32, (pitch, qb), 0) < NSA_DH
    outs = []
    for r in range(rep):
        cols = slice(r * qb, (r + 1) * qb)
        o_r = (gates[3 * r:3 * r + 1] * o_cmp[:, cols] + gates[3 * r + 1:3 * r + 2] * o_slc[:, cols]
               + gates[3 * r + 2:3 * r + 3] * o_win[:, cols])
        outs.append(jnp.where(head_lane, o_r, 0.0).T)
    o_ref[...] = jnp.concatenate(outs, axis=1).astype(o_ref.dtype)


def _nsa_attention(kvq, gates_t, k_cmp, v_cmp_t, vs_t, vw_t, batch, seq):
    t = kvq.shape[0]
    g, rep, pitch = NSA_GROUPS, NSA_REP, HEAD_PITCH
    nq = seq // Q_BLOCK
    n_cmp_pad = k_cmp.shape[2]
    nsel = seq // SLC_BLOCK
    nsel_pad = max(LANES, nsel)
    ci = np.arange(n_cmp_pad)[None, :]
    sj = np.arange(nsel_pad)[:, None]
    overlap_t = ((ci * CMP_STRIDE < (sj + 1) * SLC_BLOCK) & (ci * CMP_STRIDE + CMP_LEN > sj * SLC_BLOCK)
                 & (ci < n_cmp_pad - 1) & (sj < nsel))
    ovl = jnp.asarray(np.concatenate([overlap_t, np.ones((8, n_cmp_pad), bool)]), BF16)
    kc_len = min(SLC_CHUNK, seq)
    key_block = jnp.asarray((np.arange(seq) // SLC_BLOCK)[:, None] == np.arange(nsel_pad)[None, :], BF16)
    kern = functools.partial(_nsa_attn_kernel, seq=seq, top_k=min(SLC_TOPK, nsel))
    per_bg = lambda b, gg, i: (b, gg, 0, 0)
    fixed = lambda b, gg, i: (0, 0)

    def keys(idx):
        return pl.BlockSpec((seq, pitch), lambda b, gg, i: (b, idx * g + gg))

    values = pl.BlockSpec((1, 1, seq // LANES, pitch, LANES), lambda b, gg, i: (b, gg, 0, 0, 0))
    return pl.pallas_call(
        kern,
        grid=(batch, g, nq),
        in_specs=[
            pl.BlockSpec((Q_BLOCK, rep * pitch), lambda b, gg, i: (b * nq + i, 6 + gg)),
            pl.BlockSpec((1, 1, 3 * rep, Q_BLOCK), lambda b, gg, i: (b, gg, 0, i)),
            pl.BlockSpec((1, 1, n_cmp_pad, pitch), per_bg),
            pl.BlockSpec((1, 1, pitch, n_cmp_pad), per_bg),
            keys(2), values, keys(4), values,
            pl.BlockSpec(ovl.shape, fixed),
            pl.BlockSpec(key_block.shape, fixed),
        ],
        out_specs=pl.BlockSpec((Q_BLOCK, rep * pitch), lambda b, gg, i: (b * nq + i, gg)),
        out_shape=jax.ShapeDtypeStruct((t, NSA_HEADS * pitch), BF16),
        scratch_shapes=[pltpu.VMEM((1, rep * Q_BLOCK), F32), pltpu.VMEM((pitch, rep * Q_BLOCK), F32),
                        pltpu.VMEM((2, kc_len, rep * Q_BLOCK), F32)],
        compiler_params=_cparams(("parallel", "parallel", "arbitrary")),
        name="nsa_attention",
    )(kvq, gates_t, k_cmp, v_cmp_t, kvq, vs_t, kvq, vw_t, ovl, key_block)


def _moe_layer(h1, ridx, rw, wg, wu, wd, ln_g, ln_b):
    tile_expert, row_token, n_used, pos0, pos1, w0, w1 = _moe_plan(ridx, rw)
    ys = _experts(h1, tile_expert, row_token, n_used, wg, wu, wd)
    return _combine(ys, h1, pos0, pos1, w0, w1, ln_g, ln_b)


def kernel(x, ret_w_in, ret_w_out, nsa_w_kv, cmp_pe_k, cmp_pe_v, cmp_k_w1, cmp_k_w2, cmp_v_w1, cmp_v_w2,
           nsa_w_in, nsa_w_out, router_w, router_b, moe_w_gate, moe_w_up, moe_w_down,
           ln_mix_g, ln_mix_b, ln_ffn_g, ln_ffn_b):
    batch, seq, d = x.shape
    t = batch * seq
    h = x.reshape(t, d)

    rwt = router_w.T
    rwt_hi = rwt.astype(BF16)
    rwt_lo = (rwt - rwt_hi.astype(F32)).astype(BF16)
    rb = router_b.reshape(N_EXPERTS, 1).astype(F32)
    row = lambda v: v.reshape(1, d)

    proj = _ret_proj(h, ret_w_in[0].astype(BF16), seq)
    gated = _ret_core(proj, batch, seq)
    h1, ridx, rw = _mix_out(gated, ret_w_out[0].astype(BF16), h, row(ln_mix_g[0]), row(ln_mix_b[0]),
                            rwt_hi, rwt_lo, rb)
    h = _moe_layer(h1, ridx, rw, moe_w_gate[0].astype(BF16), moe_w_up[0].astype(BF16),
                   moe_w_down[0].astype(BF16), row(ln_ffn_g[0]), row(ln_ffn_b[0]))

    g, dh, rep, pitch = NSA_GROUPS, NSA_DH, NSA_REP, HEAD_PITCH
    n_q = NSA_HEADS * dh

    def pitch_cols(w, n_heads):
        return jnp.pad(w.reshape(d, n_heads, dh), ((0, 0), (0, 0), (0, pitch - dh))).reshape(d, n_heads * pitch)

    w_all = jnp.concatenate([pitch_cols(nsa_w_kv, 6 * g), pitch_cols(nsa_w_in[0][:, :n_q], NSA_HEADS)],
                            axis=1).astype(BF16)
    kvq = _nsa_proj(h, w_all, seq)
    w_gate = jnp.pad(nsa_w_in[0][:, n_q:], ((0, 0), (0, LANES - 3 * NSA_HEADS))).astype(BF16)
    gates = _nsa_gates(h, w_gate)[:, :3 * NSA_HEADS]
    gates_t = gates.reshape(batch, seq, g, 3 * rep).transpose(0, 2, 3, 1)

    def blocks16(p):
        p = p.reshape(batch, seq // CMP_STRIDE, CMP_STRIDE, g, pitch)[..., :dh]
        return p.transpose(0, 3, 1, 2, 4).reshape(batch, g, seq // CMP_STRIDE, CMP_STRIDE * dh)

    gw = g * pitch
    xkv = jnp.stack([blocks16(kvq[:, :gw]), blocks16(kvq[:, gw:2 * gw])])
    half_w = CMP_STRIDE * dh
    pe = jnp.stack([cmp_pe_k.reshape(2, half_w), cmp_pe_v.reshape(2, half_w)]).astype(F32)
    w1 = jnp.stack([cmp_k_w1.reshape(2, half_w, CMP_HIDDEN), cmp_v_w1.reshape(2, half_w, CMP_HIDDEN)]).astype(BF16)
    w2 = jnp.pad(jnp.stack([cmp_k_w2, cmp_v_w2]), ((0, 0), (0, 0), (0, pitch - dh))).astype(BF16)
    kv_cmp = _compress(xkv, pe, w1, w2)

    def keys_on_lanes(p):
        return p.reshape(batch, seq // LANES, LANES, g, pitch).transpose(0, 3, 1, 4, 2)

    attn = _nsa_attention(kvq, gates_t, kv_cmp[0], kv_cmp[1].transpose(0, 1, 3, 2),
                          keys_on_lanes(kvq[:, 3 * gw:4 * gw]), keys_on_lanes(kvq[:, 5 * gw:6 * gw]),
                          batch, seq)
    w_out = jnp.pad(nsa_w_out[0].reshape(NSA_HEADS, dh, d), ((0, 0), (0, pitch - dh), (0, 0))).reshape(
        NSA_HEADS * pitch, d).astype(BF16)
    h1, ridx, rw = _mix_out(attn, w_out, h, row(ln_mix_g[1]), row(ln_mix_b[1]), rwt_hi, rwt_lo, rb)
    h = _moe_layer(h1, ridx, rw, moe_w_gate[1].astype(BF16), moe_w_up[1].astype(BF16),
                   moe_w_down[1].astype(BF16), row(ln_ffn_g[1]), row(ln_ffn_b[1]))
    return h.reshape(batch, seq, d)
```

```python
import functools
import math

import jax
import jax.numpy as jnp
import numpy as np
from jax import lax
from jax.experimental import pallas as pl
from jax.experimental.pallas import tpu as pltpu

F32 = jnp.float32
BF16 = jnp.bfloat16

D_MODEL = 1024
DEPTH = 2
ALPHA = (2.0 * DEPTH) ** 0.25
LN_EPS = 1e-5
ROPE_THETA = 10000.0
NEG = -1e30

RET_HEADS = 4
RET_DK = D_MODEL // RET_HEADS
RET_DV = 2 * RET_DK
RET_CHUNK = 256

NSA_HEADS = 16
NSA_GROUPS = 4
NSA_REP = NSA_HEADS // NSA_GROUPS
NSA_DH = D_MODEL // NSA_HEADS
KV_W = NSA_GROUPS * NSA_DH
CMP_LEN = 32
CMP_STRIDE = 16
CMP_HIDDEN = 256
SLC_BLOCK = 64
SLC_TOPK = 16
WINDOW = 512
Q_BLOCK = 128
FORCE_BONUS = 1e3
SLC_CHUNK = 512
SLC_SUBCHUNK = 256

N_EXPERTS = 16
N_EXPERT_GROUPS = 4
EXPERTS_PER_GROUP = N_EXPERTS // N_EXPERT_GROUPS
N_PAIRS = EXPERTS_PER_GROUP * (EXPERTS_PER_GROUP - 1) // 2
N_CLASSES = N_EXPERT_GROUPS * N_PAIRS
D_EXPERT = 512

LANES = 128
HEAD_PITCH = LANES
ROW_TILE = 512
MOE_TILE = 256
VMEM_LIMIT = 56 * 1024 * 1024


def _cparams(sem):
    return pltpu.CompilerParams(dimension_semantics=sem, vmem_limit_bytes=VMEM_LIMIT)


def _nt_dot(a, b):
    return lax.dot_general(a, b, (((1,), (1,)), ((), ())), preferred_element_type=F32)


def _dot(a, b):
    return jnp.dot(a, b, preferred_element_type=F32)


def _rope_tables_half(seq, d):
    inv = ROPE_THETA ** (-jnp.arange(0, d, 2, dtype=F32) / d)
    ang = jnp.arange(seq, dtype=F32)[:, None] * inv[None, :]
    return jnp.cos(ang), jnp.sin(ang)


def _rope_tables_pitched(seq, d, pitch, width):
    cos, sin = _rope_tables_half(seq, d)
    zero = jnp.zeros_like(sin)
    pad = jnp.zeros((seq, pitch - d), F32)
    reps = width // pitch
    cos_t = jnp.tile(jnp.concatenate([cos, cos, pad], -1), (1, reps))
    sin_lo = jnp.tile(jnp.concatenate([-sin, zero, pad], -1), (1, reps))
    sin_hi = jnp.tile(jnp.concatenate([zero, sin, pad], -1), (1, reps))
    return cos_t, sin_lo, sin_hi


def _ret_proj_kernel(x_ref, w_ref, cos_ref, sin_ref, o_ref, *, n_rope_tiles, k_scale):
    j = pl.program_id(1)
    acc = _dot(x_ref[...].astype(BF16), w_ref[...])
    tn = acc.shape[1]
    half = RET_DK // 2

    @pl.when(j < n_rope_tiles)
    def _():
        c = cos_ref[...]
        s = sin_ref[...]
        scale = jnp.where(j >= n_rope_tiles // 2, k_scale, 1.0).astype(F32)
        for hh in range(tn // RET_DK):
            lo = hh * RET_DK
            x1 = acc[:, lo:lo + half]
            x2 = acc[:, lo + half:lo + RET_DK]
            o_ref[:, lo:lo + half] = ((x1 * c - x2 * s) * scale).astype(o_ref.dtype)
            o_ref[:, lo + half:lo + RET_DK] = ((x2 * c + x1 * s) * scale).astype(o_ref.dtype)

    @pl.when(j >= n_rope_tiles)
    def _():
        o_ref[...] = acc.astype(o_ref.dtype)


def _ret_proj(x2d, w_bf16, seq):
    t, d = x2d.shape
    n = w_bf16.shape[1]
    tm, tn = ROW_TILE, 1024
    cos, sin = _rope_tables_half(seq, RET_DK)
    tiles_per_seq = seq // tm
    n_rope_tiles = 2 * RET_HEADS * RET_DK // tn
    kern = functools.partial(_ret_proj_kernel, n_rope_tiles=n_rope_tiles, k_scale=RET_DK ** -0.5)
    return pl.pallas_call(
        kern,
        grid=(t // tm, n // tn),
        in_specs=[
            pl.BlockSpec((tm, d), lambda i, j: (i, 0)),
            pl.BlockSpec((d, tn), lambda i, j: (0, j)),
            pl.BlockSpec((tm, RET_DK // 2), lambda i, j: (i % tiles_per_seq, 0)),
            pl.BlockSpec((tm, RET_DK // 2), lambda i, j: (i % tiles_per_seq, 0)),
        ],
        out_specs=pl.BlockSpec((tm, tn), lambda i, j: (i, j)),
        out_shape=jax.ShapeDtypeStruct((t, n), BF16),
        compiler_params=_cparams(("parallel", "arbitrary")),
        name="ret_proj",
    )(x2d, w_bf16, cos, sin)


def _ret_core_kernel(q_ref, k_ref, v_ref, g_ref, mask_ref, qd_ref, kd_ref, cd_ref, o_ref, state_ref):
    c = pl.program_id(2)

    @pl.when(c == 0)
    def _():
        state_ref[...] = jnp.zeros_like(state_ref)

    q = q_ref[...]
    k = k_ref[...]
    v = v_ref[...]
    inner = _nt_dot(q, k) * mask_ref[0]
    state = state_ref[...]
    qs = (q.astype(F32) * qd_ref[0]).astype(BF16)
    o = _dot(inner.astype(BF16), v) + _dot(qs, state.astype(BF16))
    ks = k.astype(F32) * kd_ref[0]
    state_ref[...] = state * cd_ref[0] + _dot(ks.T.astype(BF16), v)

    mu = jnp.mean(o, axis=-1, keepdims=True)
    var = jnp.mean(jnp.square(o - mu), axis=-1, keepdims=True)
    on = (o - mu) * lax.rsqrt(var + LN_EPS)
    g = g_ref[...].astype(F32)
    o_ref[...] = (g * jax.nn.sigmoid(g) * on).astype(o_ref.dtype)


def _ret_core(proj, batch, seq):
    t = proj.shape[0]
    h, dk, dv, c = RET_HEADS, RET_DK, RET_DV, min(RET_CHUNK, seq)
    nc = seq // c
    log_gamma = jnp.log1p(-(2.0 ** (-5.0 - jnp.arange(h, dtype=F32))))
    idx = jnp.arange(c, dtype=F32)
    diff = idx[:, None] - idx[None, :]
    mask = jnp.where(diff >= 0, jnp.exp(log_gamma[:, None, None] * jnp.maximum(diff, 0.0)), 0.0)
    qd = jnp.broadcast_to(jnp.exp(log_gamma[:, None] * (idx[None, :] + 1.0))[:, :, None], (h, c, dk))
    kd = jnp.broadcast_to(jnp.exp(log_gamma[:, None] * (c - 1.0 - idx[None, :]))[:, :, None], (h, c, dk))
    cd = jnp.broadcast_to(jnp.exp(log_gamma * c)[:, None, None], (h, 1, dv))
    v_off = 2 * h * dk // dv
    g_off = v_off + h
    return pl.pallas_call(
        _ret_core_kernel,
        grid=(batch, h, nc),
        in_specs=[
            pl.BlockSpec((c, dk), lambda b, hh, cc: (b * nc + cc, hh)),
            pl.BlockSpec((c, dk), lambda b, hh, cc: (b * nc + cc, h + hh)),
            pl.BlockSpec((c, dv), lambda b, hh, cc: (b * nc + cc, v_off + hh)),
            pl.BlockSpec((c, dv), lambda b, hh, cc: (b * nc + cc, g_off + hh)),
            pl.BlockSpec((1, c, c), lambda b, hh, cc: (hh, 0, 0)),
            pl.BlockSpec((1, c, dk), lambda b, hh, cc: (hh, 0, 0)),
            pl.BlockSpec((1, c, dk), lambda b, hh, cc: (hh, 0, 0)),
            pl.BlockSpec((1, 1, dv), lambda b, hh, cc: (hh, 0, 0)),
        ],
        out_specs=pl.BlockSpec((c, dv), lambda b, hh, cc: (b * nc + cc, hh)),
        out_shape=jax.ShapeDtypeStruct((t, h * dv), BF16),
        scratch_shapes=[pltpu.VMEM((dk, dv), F32)],
        compiler_params=_cparams(("parallel", "parallel", "arbitrary")),
        name="ret_core",
    )(proj, proj, proj, proj, mask, qd, kd, cd)


def _layer_norm_rows(y, g, b):
    mu = jnp.mean(y, axis=-1, keepdims=True)
    var = jnp.mean(jnp.square(y - mu), axis=-1, keepdims=True)
    return (y - mu) * lax.rsqrt(var + LN_EPS) * g + b


def _split_bf16(x):
    hi = x.astype(BF16)
    lo = (x - hi.astype(F32)).astype(BF16)
    return hi, lo


def _route(hn, rwt_hi_ref, rwt_lo_ref, rb_ref, cls_ref):
    h_hi, h_lo = _split_bf16(hn)
    w_hi = rwt_hi_ref[...]
    logits = _nt_dot(w_hi, h_hi) + _nt_dot(w_hi, h_lo) + _nt_dot(rwt_lo_ref[...], h_hi)
    aff = jax.nn.sigmoid(logits)
    sel = aff + rb_ref[...]
    tm = hn.shape[0]

    def row(a, e):
        return a[e:e + 1, :]

    best_g = jnp.zeros((1, tm), jnp.int32)
    best_s = None
    for gi in range(N_EXPERT_GROUPS):
        a0, a1, a2, a3 = [row(sel, EXPERTS_PER_GROUP * gi + e) for e in range(EXPERTS_PER_GROUP)]
        hi01, lo01 = jnp.maximum(a0, a1), jnp.minimum(a0, a1)
        hi23, lo23 = jnp.maximum(a2, a3), jnp.minimum(a2, a3)
        top2 = jnp.maximum(hi01, hi23) + jnp.maximum(jnp.minimum(hi01, hi23), jnp.maximum(lo01, lo23))
        if best_s is None:
            best_s = top2
        else:
            upd = top2 > best_s
            best_g = jnp.where(upd, gi, best_g)
            best_s = jnp.where(upd, top2, best_s)

    def pick(a, e):
        out = row(a, e)
        for gi in range(1, N_EXPERT_GROUPS):
            out = jnp.where(best_g == gi, row(a, EXPERTS_PER_GROUP * gi + e), out)
        return out

    cs = [pick(sel, e) for e in range(EXPERTS_PER_GROUP)]
    af = [pick(aff, e) for e in range(EXPERTS_PER_GROUP)]
    i1 = jnp.zeros((1, tm), jnp.int32)
    s1, a1v = cs[0], af[0]
    for e in range(1, EXPERTS_PER_GROUP):
        upd = cs[e] > s1
        i1 = jnp.where(upd, e, i1)
        s1 = jnp.where(upd, cs[e], s1)
        a1v = jnp.where(upd, af[e], a1v)
    i2 = jnp.full((1, tm), -1, jnp.int32)
    s2 = jnp.full((1, tm), -jnp.inf, F32)
    a2v = jnp.zeros((1, tm), F32)
    for e in range(EXPERTS_PER_GROUP):
        upd = (i1 != e) & ((i2 < 0) | (cs[e] > s2))
        i2 = jnp.where(upd, e, i2)
        s2 = jnp.where(upd, cs[e], s2)
        a2v = jnp.where(upd, af[e], a2v)
    tot = a1v + a2v
    w1, w2 = a1v / tot, a2v / tot
    first_is_lo = i1 < i2
    lo = jnp.where(first_is_lo, i1, i2)
    hi = jnp.where(first_is_lo, i2, i1)
    pair = jnp.right_shift(lo * (2 * EXPERTS_PER_GROUP - 1 - lo), 1) + (hi - lo - 1)
    cls = best_g * N_PAIRS + pair
    cls_ref[...] = jnp.concatenate([cls, jnp.zeros((7, tm), jnp.int32)], axis=0)
    w_lo = jnp.where(first_is_lo, w1, w2)
    w_hi_ = jnp.where(first_is_lo, w2, w1)
    return jnp.concatenate([w_lo, w_hi_, jnp.zeros((LANES - 2, tm), F32)], axis=0).T


def _mix_out_kernel(a_ref, w_ref, h_ref, g_ref, b_ref, rwt_hi_ref, rwt_lo_ref, rb_ref, o_ref, cls_ref):
    mix = _dot(a_ref[...], w_ref[...])
    hn = _layer_norm_rows(ALPHA * h_ref[...] + mix, g_ref[...], b_ref[...])
    d = hn.shape[1]
    o_ref[:, :d] = hn
    o_ref[:, d:] = _route(hn, rwt_hi_ref, rwt_lo_ref, rb_ref, cls_ref)


def _mix_out(a_bf16, w_bf16, h2d, ln_g, ln_b, rwt_hi, rwt_lo, rb):
    t = a_bf16.shape[0]
    d = h2d.shape[1]
    k = a_bf16.shape[1]
    tm = ROW_TILE
    row = lambda i: (i, 0)
    fixed = lambda i: (0, 0)
    return pl.pallas_call(
        _mix_out_kernel,
        grid=(t // tm,),
        in_specs=[
            pl.BlockSpec((tm, k), row),
            pl.BlockSpec((k, d), fixed),
            pl.BlockSpec((tm, d), row),
            pl.BlockSpec((1, d), fixed),
            pl.BlockSpec((1, d), fixed),
            pl.BlockSpec((N_EXPERTS, d), fixed),
            pl.BlockSpec((N_EXPERTS, d), fixed),
            pl.BlockSpec((N_EXPERTS, 1), fixed),
        ],
        out_specs=[
            pl.BlockSpec((tm, d + LANES), row),
            pl.BlockSpec((8, tm), lambda i: (0, i)),
        ],
        out_shape=[
            jax.ShapeDtypeStruct((t, d + LANES), F32),
            jax.ShapeDtypeStruct((8, t), jnp.int32),
        ],
        compiler_params=_cparams(("parallel",)),
        name="mix_out",
    )(a_bf16, w_bf16, h2d, ln_g, ln_b, rwt_hi, rwt_lo, rb)


def _expert_kernel(ea_ref, eb_ref, dst_ref, nused_ref, h_hbm, wga_ref, wua_ref, wda_ref, wgb_ref, wub_ref,
                   wdb_ref, g_ref, b_ref, o_hbm, xbuf, ybuf, gsem, ssem, *, n_tokens):
    i = pl.program_id(0)
    tile = xbuf.shape[1]
    d = ybuf.shape[2]
    n_used = nused_ref[0]

    def start_gather(step, slot):
        base = step * tile
        for r in range(tile):
            src = jnp.minimum(dst_ref[base + r], n_tokens - 1)
            pltpu.make_async_copy(h_hbm.at[pl.ds(src, 1)], xbuf.at[slot, pl.ds(r, 1)], gsem.at[slot]).start()

    def start_scatter(step, slot):
        base = step * tile
        for r in range(tile):
            dst = dst_ref[base + r]
            pltpu.make_async_copy(ybuf.at[slot, pl.ds(r, 1)], o_hbm.at[pl.ds(dst, 1)], ssem.at[slot]).start()

    def wait_gather(slot):
        pltpu.make_async_copy(h_hbm.at[pl.ds(0, tile)], xbuf.at[slot], gsem.at[slot]).wait()

    def wait_scatter(slot):
        pltpu.make_async_copy(ybuf.at[slot], o_hbm.at[pl.ds(0, tile)], ssem.at[slot]).wait()

    @pl.when(i == 0)
    def _():
        ybuf[...] = jnp.zeros_like(ybuf)
        for slot in range(2):
            fill = pltpu.make_async_copy(ybuf.at[slot], o_hbm.at[pl.ds(n_tokens + slot * tile, tile)], ssem.at[slot])
            fill.start()
            fill.wait()
        start_gather(i, 0)

    def step(slot):
        @pl.when(i + 1 < n_used)
        def _():
            start_gather(i + 1, 1 - slot)

        wait_gather(slot)

        @pl.when(i >= 2)
        def _():
            wait_scatter(slot)

        xe = xbuf[slot]
        x = xe[:, :d]
        xb = x.astype(BF16)

        def expert(wg_ref, wu_ref, wd_ref):
            gate = _dot(xb, wg_ref[0])
            up = _dot(xb, wu_ref[0])
            return _dot((gate * jax.nn.sigmoid(gate) * up).astype(BF16), wd_ref[0])

        y = xe[:, d:d + 1] * expert(wga_ref, wua_ref, wda_ref) + xe[:, d + 1:d + 2] * expert(wgb_ref, wub_ref, wdb_ref)
        ybuf[slot] = _layer_norm_rows(ALPHA * x + y, g_ref[...], b_ref[...])
        start_scatter(i, slot)

        @pl.when(i == n_used - 1)
        def _():
            @pl.when(i >= 1)
            def _():
                wait_scatter(1 - slot)
            wait_scatter(slot)

    @pl.when((i < n_used) & (i % 2 == 0))
    def _():
        step(0)

    @pl.when((i < n_used) & (i % 2 == 1))
    def _():
        step(1)


def _experts(h1x, tile_ea, tile_eb, row_dst, n_used, wg, wu, wd, ln_g, ln_b, n_tokens):
    d = wg.shape[1]
    de = wg.shape[2]
    n_tiles = tile_ea.shape[0]
    ea = lambda i, a, b, rd, nu: (a[i], 0, 0)
    eb = lambda i, a, b, rd, nu: (b[i], 0, 0)
    fixed = lambda i, a, b, rd, nu: (0, 0)
    grid_spec = pltpu.PrefetchScalarGridSpec(
        num_scalar_prefetch=4,
        grid=(n_tiles,),
        in_specs=[
            pl.BlockSpec(memory_space=pl.ANY),
            pl.BlockSpec((1, d, de), ea), pl.BlockSpec((1, d, de), ea), pl.BlockSpec((1, de, d), ea),
            pl.BlockSpec((1, d, de), eb), pl.BlockSpec((1, d, de), eb), pl.BlockSpec((1, de, d), eb),
            pl.BlockSpec((1, d), fixed), pl.BlockSpec((1, d), fixed),
        ],
        out_specs=pl.BlockSpec(memory_space=pl.ANY),
        scratch_shapes=[pltpu.VMEM((2, MOE_TILE, h1x.shape[1]), F32), pltpu.VMEM((2, MOE_TILE, d), F32),
                        pltpu.SemaphoreType.DMA((2,)), pltpu.SemaphoreType.DMA((2,))],
    )
    return pl.pallas_call(
        functools.partial(_expert_kernel, n_tokens=n_tokens),
        grid_spec=grid_spec,
        out_shape=jax.ShapeDtypeStruct((n_tokens + 2 * MOE_TILE, d), F32),
        compiler_params=_cparams(("arbitrary",)),
        name="moe_experts",
    )(tile_ea, tile_eb, row_dst, n_used, h1x, wg, wu, wd, wg, wu, wd, ln_g, ln_b)


def _moe_plan(cls):
    t = cls.shape[0]
    onehot = (cls[:, None] == jnp.arange(N_CLASSES)[None, :]).astype(jnp.int32)
    csum = jnp.cumsum(onehot, axis=0)
    rank = jnp.take_along_axis(csum, cls[:, None], axis=1)[:, 0] - 1
    counts = csum[-1]
    padded = (counts + MOE_TILE - 1) // MOE_TILE * MOE_TILE
    ends = jnp.cumsum(padded)
    pos = (ends - padded)[cls] + rank
    n_tiles = t // MOE_TILE + N_CLASSES
    row = jnp.arange(n_tiles * MOE_TILE, dtype=jnp.int32)
    padding_dst = t + (row // MOE_TILE % 2) * MOE_TILE + row % MOE_TILE
    row_dst = padding_dst.at[pos].set(jnp.arange(t, dtype=jnp.int32))
    tile_start = jnp.arange(n_tiles, dtype=jnp.int32) * MOE_TILE
    tile_cls = jnp.minimum(jnp.sum((tile_start[:, None] >= ends[None, :]).astype(jnp.int32), axis=1), N_CLASSES - 1)
    pair_lo = np.array([a for a in range(EXPERTS_PER_GROUP) for b in range(a + 1, EXPERTS_PER_GROUP)], np.int32)
    pair_hi = np.array([b for a in range(EXPERTS_PER_GROUP) for b in range(a + 1, EXPERTS_PER_GROUP)], np.int32)
    base = tile_cls // N_PAIRS * EXPERTS_PER_GROUP
    tile_ea = (base + jnp.asarray(pair_lo)[tile_cls % N_PAIRS]).astype(jnp.int32)
    tile_eb = (base + jnp.asarray(pair_hi)[tile_cls % N_PAIRS]).astype(jnp.int32)
    n_used = (ends[-1] // MOE_TILE).astype(jnp.int32).reshape(1)
    return tile_ea, tile_eb, row_dst, n_used


def _nsa_proj_kernel(x_ref, w_ref, cos_ref, slo_ref, shi_ref, o_ref, *, n_kv_tiles, q_scale):
    j = pl.program_id(1)
    acc = _dot(x_ref[...].astype(BF16), w_ref[...])
    tn = acc.shape[1]
    half = NSA_DH // 2
    is_q = j >= n_kv_tiles
    roped = is_q | (j % 2 == 0)

    @pl.when(roped)
    def _():
        rot = (pltpu.roll(acc, tn - half, axis=1) * slo_ref[...] + pltpu.roll(acc, half, axis=1) * shi_ref[...])
        scale = jnp.where(is_q, q_scale, 1.0).astype(F32)
        o_ref[...] = ((acc * cos_ref[...] + rot) * scale).astype(o_ref.dtype)

    @pl.when(jnp.logical_not(roped))
    def _():
        lane = lax.broadcasted_iota(jnp.int32, (1, tn), 1)
        o_ref[...] = (acc + jnp.where(lane % HEAD_PITCH == NSA_DH, 1.0, 0.0)).astype(o_ref.dtype)


def _nsa_proj(x2d, w_bf16, seq, t):
    d = x2d.shape[1]
    n = w_bf16.shape[1]
    tm, tn = ROW_TILE, NSA_GROUPS * HEAD_PITCH
    cos_t, sin_lo, sin_hi = _rope_tables_pitched(seq, NSA_DH, HEAD_PITCH, tn)
    tiles_per_seq = seq // tm
    tab = lambda i, j: (i % tiles_per_seq, 0)
    kern = functools.partial(_nsa_proj_kernel, n_kv_tiles=6, q_scale=NSA_DH ** -0.5 * math.log2(math.e))
    return pl.pallas_call(
        kern,
        grid=(t // tm, n // tn),
        in_specs=[
            pl.BlockSpec((tm, d), lambda i, j: (i, 0)),
            pl.BlockSpec((d, tn), lambda i, j: (0, j)),
            pl.BlockSpec((tm, tn), tab),
            pl.BlockSpec((tm, tn), tab),
            pl.BlockSpec((tm, tn), tab),
        ],
        out_specs=pl.BlockSpec((tm, tn), lambda i, j: (i, j)),
        out_shape=jax.ShapeDtypeStruct((t, n), BF16),
        compiler_params=_cparams(("parallel", "arbitrary")),
        name="nsa_proj",
    )(x2d, w_bf16, cos_t, sin_lo, sin_hi)


def _gate_kernel(x_ref, w_ref, o_ref):
    o_ref[...] = jax.nn.sigmoid(_dot(x_ref[...].astype(BF16), w_ref[...]))


def _nsa_gates(x2d, wg_bf16, t):
    d = x2d.shape[1]
    tm = ROW_TILE
    return pl.pallas_call(
        _gate_kernel,
        grid=(t // tm,),
        in_specs=[pl.BlockSpec((tm, d), lambda i: (i, 0)), pl.BlockSpec((d, LANES), lambda i: (0, 0))],
        out_specs=pl.BlockSpec((tm, LANES), lambda i: (i, 0)),
        out_shape=jax.ShapeDtypeStruct((t, LANES), F32),
        compiler_params=_cparams(("parallel",)),
        name="nsa_gates",
    )(x2d, wg_bf16)


def _gelu_tanh(x):
    return 0.5 * x * (1.0 + jnp.tanh(math.sqrt(2.0 / math.pi) * (x + 0.044715 * (x * x * x))))


def _cmp_kernel(x_ref, pe_ref, w1_ref, w2_ref, o_ref):
    x = x_ref[0, 0, 0].astype(F32)
    n = x.shape[0]
    xa = (x + pe_ref[0, 0:1, :]).astype(BF16)
    xb = (x + pe_ref[0, 1:2, :]).astype(BF16)
    u = _dot(xa, w1_ref[0, 0])
    v = _dot(xb, w1_ref[0, 1])
    pre = u + pltpu.roll(v, n - 1, axis=0)
    out = _dot(_gelu_tanh(pre).astype(BF16), w2_ref[0])
    lane = lax.broadcasted_iota(jnp.int32, (1, out.shape[1]), 1)
    ones_lane = jnp.where((lane == NSA_DH) & (pl.program_id(0) == 1), 1.0, 0.0)
    o_ref[0, 0, 0] = (out + ones_lane).astype(o_ref.dtype)


def _compress(xkv, pe, w1, w2):
    _, b, g, n, w = xkv.shape
    pitch = w2.shape[2]
    return pl.pallas_call(
        _cmp_kernel,
        grid=(2, b, g),
        in_specs=[
            pl.BlockSpec((1, 1, 1, n, w), lambda s, bb, gg: (s, bb, gg, 0, 0)),
            pl.BlockSpec((1, 2, w), lambda s, bb, gg: (s, 0, 0)),
            pl.BlockSpec((1, 2, w, CMP_HIDDEN), lambda s, bb, gg: (s, 0, 0, 0)),
            pl.BlockSpec((1, CMP_HIDDEN, pitch), lambda s, bb, gg: (s, 0, 0)),
        ],
        out_specs=pl.BlockSpec((1, 1, 1, n, pitch), lambda s, bb, gg: (s, bb, gg, 0, 0)),
        out_shape=jax.ShapeDtypeStruct((2, b, g, n, pitch), BF16),
        compiler_params=_cparams(("parallel", "parallel", "parallel")),
        name="nsa_compress",
    )(xkv, pe, w1, w2)


def _softmax_pv(s_t, v_t):
    m = jnp.max(s_t, axis=0, keepdims=True)
    p = jnp.exp2(s_t - m).astype(BF16)
    return p, _dot(v_t, p)


def _normalize(ov_t):
    return ov_t * (1.0 / ov_t[NSA_DH:NSA_DH + 1, :])


def _nsa_attn_kernel(q_ref, gate_ref, kc_ref, vc_ref, ks_ref, vs_ref, kw_ref, vw_ref, ovl_ref, xt_ref,
                     o_ref, m_ref, acc_ref, s_ref, *, seq, top_k):
    i = pl.program_id(2)
    qb, rep, pitch = Q_BLOCK, NSA_REP, HEAD_PITCH
    t0 = i * qb
    q2 = q_ref[...]
    q4 = jnp.concatenate([q2[:, r * pitch:(r + 1) * pitch] for r in range(rep)], axis=0)
    t_lane = t0 + lax.broadcasted_iota(jnp.int32, (1, qb), 1)
    t_lane4 = jnp.concatenate([t_lane] * rep, axis=1)

    def add_cols(s_t, bias_t):
        return jnp.concatenate([s_t[:, r * qb:(r + 1) * qb] + bias_t for r in range(rep)], axis=1)

    n_cmp_pad = kc_ref.shape[2]
    cmp_end = lax.broadcasted_iota(jnp.int32, (n_cmp_pad, qb), 0) * CMP_STRIDE + (CMP_LEN - 1)
    s_c = add_cols(_nt_dot(kc_ref[0, 0], q4), jnp.where(cmp_end <= t_lane, 0.0, NEG))
    p_c, ov_c = _softmax_pv(s_c, vc_ref[0, 0])
    o_cmp = jnp.where(t_lane4 >= CMP_LEN - 1, _normalize(ov_c), 0.0)

    ovl = ovl_ref[...]
    nsel = ovl.shape[0] - 8
    imp = jnp.zeros((nsel, qb), F32)
    for r in range(rep):
        ext = _dot(ovl, p_c[:, r * qb:(r + 1) * qb])
        imp = imp + ext[:nsel] * (1.0 / ext[nsel:nsel + 1])

    def values_t(v_ref, first_tile, n_tiles):
        return jnp.concatenate([v_ref[0, 0, first_tile + u] for u in range(n_tiles)], axis=1)

    wk = min(seq, qb + WINDOW)
    w_start = pl.multiple_of(jnp.maximum(t0 + qb - wk, 0), qb)
    dist = t_lane - (w_start + lax.broadcasted_iota(jnp.int32, (wk, qb), 0))
    bias_w = jnp.where((dist >= 0) & (dist < WINDOW), 0.0, NEG)
    s_w = add_cols(_nt_dot(kw_ref[pl.ds(w_start, wk), :], q4), bias_w)
    _, ov_w = _softmax_pv(s_w, values_t(vw_ref, w_start // LANES, wk // LANES))
    o_win = _normalize(ov_w)

    blk = lax.broadcasted_iota(jnp.int32, (nsel, qb), 0)
    t_col = t0 + lax.broadcasted_iota(jnp.int32, (nsel, qb), 1)
    cur = t_col // SLC_BLOCK
    valid = (blk * SLC_BLOCK <= t_col) & (blk < seq // SLC_BLOCK)
    forced = (blk == 0) | (blk == cur) | (blk == cur - 1)
    score = jnp.where(valid, imp + jnp.where(forced, FORCE_BONUS, 0.0), NEG)
    blk_f = blk.astype(F32)
    chosen = jnp.zeros((nsel, qb), F32)
    for _ in range(top_k):
        m = jnp.max(score, axis=0, keepdims=True)
        first = jnp.min(jnp.where(score == m, blk_f, float(nsel)), axis=0, keepdims=True)
        hit = blk_f == first
        chosen = jnp.where(hit, 1.0, chosen)
        score = jnp.where(hit, -jnp.inf, score)
    chosen = jnp.where(valid, chosen, 0.0)
    sel_bias = ((chosen.T - 1.0) * -NEG).astype(BF16)

    q_aug = jnp.concatenate([q4, jnp.concatenate([sel_bias] * rep, axis=0)], axis=1)
    kc_len = min(SLC_CHUNK, seq)
    last = t0 // kc_len
    k_row = lax.broadcasted_iota(jnp.int32, (kc_len, qb), 0)
    q_lane = lax.broadcasted_iota(jnp.int32, (kc_len, qb), 1) + (t0 - last * kc_len)
    causal = jnp.where(k_row <= q_lane, 0.0, NEG)

    def slc_scores(c, slot):
        start = pl.multiple_of(c * kc_len, kc_len)
        k_aug = jnp.concatenate([ks_ref[pl.ds(start, kc_len), :], xt_ref[pl.ds(start, kc_len), :]], axis=1)
        s = _nt_dot(k_aug, q_aug)
        s_ref[slot] = add_cols(s, causal * (c == last).astype(F32))

    def slc_update(c, slot, m_old, acc_old):
        s = s_ref[slot]
        m_new = jnp.maximum(m_old, jnp.max(s, axis=0, keepdims=True))
        p = jnp.exp2(s - m_new).astype(BF16)
        v_t = values_t(vs_ref, c * (kc_len // LANES), kc_len // LANES)
        return m_new, jnp.exp2(m_old - m_new) * acc_old + _dot(v_t, p)

    def chunk_pair(j, carry):
        slc_scores(2 * j + 1, 1)
        m, acc = slc_update(2 * j, 0, m_ref[...], acc_ref[...])
        slc_scores(jnp.minimum(2 * j + 2, last), 0)
        m, acc = slc_update(2 * j + 1, 1, m, acc)
        m_ref[...] = m
        acc_ref[...] = acc
        return carry

    m_ref[...] = jnp.full(m_ref.shape, NEG, F32)
    acc_ref[...] = jnp.zeros_like(acc_ref)
    slc_scores(0, 0)
    lax.fori_loop(0, (last + 1) // 2, chunk_pair, 0)

    @pl.when(last % 2 == 0)
    def _():
        m, acc = slc_update(last, 0, m_ref[...], acc_ref[...])
        m_ref[...] = m
        acc_ref[...] = acc

    o_slc = _normalize(acc_ref[...])

    gates = gate_ref[0, 0]
    head_lane = lax.broadcasted_iota(jnp.int32, (pitch, qb), 0) < NSA_DH
    outs = []
    for r in range(rep):
        cols = slice(r * qb, (r + 1) * qb)
        o_r = (gates[3 * r:3 * r + 1] * o_cmp[:, cols] + gates[3 * r + 1:3 * r + 2] * o_slc[:, cols]
               + gates[3 * r + 2:3 * r + 3] * o_win[:, cols])
        outs.append(jnp.where(head_lane, o_r, 0.0).T)
    o_ref[...] = jnp.concatenate(outs, axis=1).astype(o_ref.dtype)


def _nsa_attention(kvq, gates_t, k_cmp, v_cmp_t, vs_t, vw_t, batch, seq):
    t = kvq.shape[0]
    g, rep, pitch = NSA_GROUPS, NSA_REP, HEAD_PITCH
    nq = seq // Q_BLOCK
    n_cmp_pad = k_cmp.shape[2]
    nsel = seq // SLC_BLOCK
    nsel_pad = max(LANES, nsel)
    ci = np.arange(n_cmp_pad)[None, :]
    sj = np.arange(nsel_pad)[:, None]
    overlap_t = ((ci * CMP_STRIDE < (sj + 1) * SLC_BLOCK) & (ci * CMP_STRIDE + CMP_LEN > sj * SLC_BLOCK)
                 & (ci < n_cmp_pad - 1) & (sj < nsel))
    ovl = jnp.asarray(np.concatenate([overlap_t, np.ones((8, n_cmp_pad), bool)]), BF16)
    kc_len = min(SLC_CHUNK, seq)
    key_block = jnp.asarray((np.arange(seq) // SLC_BLOCK)[:, None] == np.arange(nsel_pad)[None, :], BF16)
    kern = functools.partial(_nsa_attn_kernel, seq=seq, top_k=min(SLC_TOPK, nsel))
    per_bg = lambda b, gg, i: (b, gg, 0, 0)
    fixed = lambda b, gg, i: (0, 0)

    def keys(idx):
        return pl.BlockSpec((seq, pitch), lambda b, gg, i: (b, idx * g + gg))

    values = pl.BlockSpec((1, 1, seq // LANES, pitch, LANES), lambda b, gg, i: (b, gg, 0, 0, 0))
    return pl.pallas_call(
        kern,
        grid=(batch, g, nq),
        in_specs=[
            pl.BlockSpec((Q_BLOCK, rep * pitch), lambda b, gg, i: (b * nq + i, 6 + gg)),
            pl.BlockSpec((1, 1, 3 * rep, Q_BLOCK), lambda b, gg, i: (b, gg, 0, i)),
            pl.BlockSpec((1, 1, n_cmp_pad, pitch), per_bg),
            pl.BlockSpec((1, 1, pitch, n_cmp_pad), per_bg),
            keys(2), values, keys(4), values,
            pl.BlockSpec(ovl.shape, fixed),
            pl.BlockSpec(key_block.shape, fixed),
        ],
        out_specs=pl.BlockSpec((Q_BLOCK, rep * pitch), lambda b, gg, i: (b * nq + i, gg)),
        out_shape=jax.ShapeDtypeStruct((t, NSA_HEADS * pitch), BF16),
        scratch_shapes=[pltpu.VMEM((1, rep * Q_BLOCK), F32), pltpu.VMEM((pitch, rep * Q_BLOCK), F32),
                        pltpu.VMEM((2, kc_len, rep * Q_BLOCK), F32)],
        compiler_params=_cparams(("parallel", "parallel", "arbitrary")),
        name="nsa_attention",
    )(kvq, gates_t, k_cmp, v_cmp_t, kvq, vs_t, kvq, vw_t, ovl, key_block)


def _moe_layer(h1x, cls, wg, wu, wd, ln_g, ln_b):
    tile_ea, tile_eb, row_dst, n_used = _moe_plan(cls[0])
    return _experts(h1x, tile_ea, tile_eb, row_dst, n_used, wg, wu, wd, ln_g, ln_b, h1x.shape[0])


def kernel(x, ret_w_in, ret_w_out, nsa_w_kv, cmp_pe_k, cmp_pe_v, cmp_k_w1, cmp_k_w2, cmp_v_w1, cmp_v_w2,
           nsa_w_in, nsa_w_out, router_w, router_b, moe_w_gate, moe_w_up, moe_w_down,
           ln_mix_g, ln_mix_b, ln_ffn_g, ln_ffn_b):
    batch, seq, d = x.shape
    t = batch * seq
    h = x.reshape(t, d)

    rwt = router_w.T
    rwt_hi = rwt.astype(BF16)
    rwt_lo = (rwt - rwt_hi.astype(F32)).astype(BF16)
    rb = router_b.reshape(N_EXPERTS, 1).astype(F32)
    row = lambda v: v.reshape(1, d)

    proj = _ret_proj(h, ret_w_in[0].astype(BF16), seq)
    gated = _ret_core(proj, batch, seq)
    h1x, cls = _mix_out(gated, ret_w_out[0].astype(BF16), h, row(ln_mix_g[0]), row(ln_mix_b[0]),
                        rwt_hi, rwt_lo, rb)
    h = _moe_layer(h1x, cls, moe_w_gate[0].astype(BF16), moe_w_up[0].astype(BF16),
                   moe_w_down[0].astype(BF16), row(ln_ffn_g[0]), row(ln_ffn_b[0]))

    g, dh, rep, pitch = NSA_GROUPS, NSA_DH, NSA_REP, HEAD_PITCH
    n_q = NSA_HEADS * dh

    def pitch_cols(w, n_heads):
        return jnp.pad(w.reshape(d, n_heads, dh), ((0, 0), (0, 0), (0, pitch - dh))).reshape(d, n_heads * pitch)

    w_all = jnp.concatenate([pitch_cols(nsa_w_kv, 6 * g), pitch_cols(nsa_w_in[0][:, :n_q], NSA_HEADS)],
                            axis=1).astype(BF16)
    kvq = _nsa_proj(h, w_all, seq, t)
    w_gate = jnp.pad(nsa_w_in[0][:, n_q:], ((0, 0), (0, LANES - 3 * NSA_HEADS))).astype(BF16)
    gates = _nsa_gates(h, w_gate, t)[:, :3 * NSA_HEADS]
    gates_t = gates.reshape(batch, seq, g, 3 * rep).transpose(0, 2, 3, 1)

    def blocks16(p):
        p = p.reshape(batch, seq // CMP_STRIDE, CMP_STRIDE, g, pitch)[..., :dh]
        return p.transpose(0, 3, 1, 2, 4).reshape(batch, g, seq // CMP_STRIDE, CMP_STRIDE * dh)

    gw = g * pitch
    xkv = jnp.stack([blocks16(kvq[:, :gw]), blocks16(kvq[:, gw:2 * gw])])
    half_w = CMP_STRIDE * dh
    pe = jnp.stack([cmp_pe_k.reshape(2, half_w), cmp_pe_v.reshape(2, half_w)]).astype(F32)
    w1 = jnp.stack([cmp_k_w1.reshape(2, half_w, CMP_HIDDEN), cmp_v_w1.reshape(2, half_w, CMP_HIDDEN)]).astype(BF16)
    w2 = jnp.pad(jnp.stack([cmp_k_w2, cmp_v_w2]), ((0, 0), (0, 0), (0, pitch - dh))).astype(BF16)
    kv_cmp = _compress(xkv, pe, w1, w2)

    def keys_on_lanes(p):
        return p.reshape(batch, seq // LANES, LANES, g, pitch).transpose(0, 3, 1, 4, 2)

    attn = _nsa_attention(kvq, gates_t, kv_cmp[0], kv_cmp[1].transpose(0, 1, 3, 2),
                          keys_on_lanes(kvq[:, 3 * gw:4 * gw]), keys_on_lanes(kvq[:, 5 * gw:6 * gw]),
                          batch, seq)
    w_out = jnp.pad(nsa_w_out[0].reshape(NSA_HEADS, dh, d), ((0, 0), (0, pitch - dh), (0, 0))).reshape(
        NSA_HEADS * pitch, d).astype(BF16)
    h1x, cls = _mix_out(attn, w_out, h, row(ln_mix_g[1]), row(ln_mix_b[1]), rwt_hi, rwt_lo, rb)
    h = _moe_layer(h1x, cls, moe_w_gate[1].astype(BF16), moe_w_up[1].astype(BF16),
                   moe_w_down[1].astype(BF16), row(ln_ffn_g[1]), row(ln_ffn_b[1]))
    return h[:t].reshape(batch, seq, d)
```

```python
import functools
import math

import jax
import jax.numpy as jnp
import numpy as np
from jax import lax
from jax.experimental import pallas as pl
from jax.experimental.pallas import tpu as pltpu

F32 = jnp.float32
BF16 = jnp.bfloat16

D_MODEL = 1024
DEPTH = 2
ALPHA = (2.0 * DEPTH) ** 0.25
LN_EPS = 1e-5
ROPE_THETA = 10000.0
NEG = -1e30

RET_HEADS = 4
RET_DK = D_MODEL // RET_HEADS
RET_DV = 2 * RET_DK
RET_CHUNK = 256

NSA_HEADS = 16
NSA_GROUPS = 4
NSA_REP = NSA_HEADS // NSA_GROUPS
NSA_DH = D_MODEL // NSA_HEADS
KV_W = NSA_GROUPS * NSA_DH
CMP_LEN = 32
CMP_STRIDE = 16
CMP_HIDDEN = 256
SLC_BLOCK = 64
SLC_TOPK = 16
WINDOW = 512
Q_BLOCK = 256
FORCE_BONUS = 1e3
SLC_CHUNK = 512
SLC_SUBCHUNK = 256

N_EXPERTS = 16
N_EXPERT_GROUPS = 4
EXPERTS_PER_GROUP = N_EXPERTS // N_EXPERT_GROUPS
N_PAIRS = EXPERTS_PER_GROUP * (EXPERTS_PER_GROUP - 1) // 2
N_CLASSES = N_EXPERT_GROUPS * N_PAIRS
D_EXPERT = 512

LANES = 128
HEAD_PITCH = LANES
V_ROWS = NSA_DH + 16
ROW_TILE = 512
MOE_TILE = 256
VMEM_LIMIT = 56 * 1024 * 1024


def _cparams(sem):
    return pltpu.CompilerParams(dimension_semantics=sem, vmem_limit_bytes=VMEM_LIMIT)


def _nt_dot(a, b):
    return lax.dot_general(a, b, (((1,), (1,)), ((), ())), preferred_element_type=F32)


def _dot(a, b):
    return jnp.dot(a, b, preferred_element_type=F32)


def _rope_tables_half(seq, d):
    inv = ROPE_THETA ** (-jnp.arange(0, d, 2, dtype=F32) / d)
    ang = jnp.arange(seq, dtype=F32)[:, None] * inv[None, :]
    return jnp.cos(ang), jnp.sin(ang)


def _split_halves(a, d, pitch):
    lead = a.shape[:-1]
    heads = a.shape[-1] // d
    a = a.reshape(lead + (heads, 2, d // 2))
    a = jnp.pad(a, [(0, 0)] * (len(lead) + 2) + [(0, (pitch - d) // 2)])
    return a.reshape(lead + (heads * pitch,))


def _rope_tables_split(seq, d, pitch, width, scale):
    cos, sin = _rope_tables_half(seq, d)
    reps = width // pitch
    cos_t = jnp.tile(_split_halves(jnp.concatenate([cos, cos], -1), d, pitch), (1, reps))
    sin_t = jnp.tile(_split_halves(jnp.concatenate([-sin, sin], -1), d, pitch), (1, reps))
    return cos_t * scale, sin_t * scale


def _ret_proj_kernel(x_ref, w_ref, cos_ref, sin_ref, o_ref, *, n_rope_tiles, k_scale):
    j = pl.program_id(1)
    acc = _dot(x_ref[...].astype(BF16), w_ref[...])
    tn = acc.shape[1]
    half = RET_DK // 2

    @pl.when(j < n_rope_tiles)
    def _():
        c = cos_ref[...]
        s = sin_ref[...]
        scale = jnp.where(j >= n_rope_tiles // 2, k_scale, 1.0).astype(F32)
        for hh in range(tn // RET_DK):
            lo = hh * RET_DK
            x1 = acc[:, lo:lo + half]
            x2 = acc[:, lo + half:lo + RET_DK]
            o_ref[:, lo:lo + half] = ((x1 * c - x2 * s) * scale).astype(o_ref.dtype)
            o_ref[:, lo + half:lo + RET_DK] = ((x2 * c + x1 * s) * scale).astype(o_ref.dtype)

    @pl.when(j >= n_rope_tiles)
    def _():
        o_ref[...] = acc.astype(o_ref.dtype)


def _ret_proj(x2d, w_bf16, seq):
    t, d = x2d.shape
    n = w_bf16.shape[1]
    tm, tn = ROW_TILE, 1024
    cos, sin = _rope_tables_half(seq, RET_DK)
    tiles_per_seq = seq // tm
    n_rope_tiles = 2 * RET_HEADS * RET_DK // tn
    kern = functools.partial(_ret_proj_kernel, n_rope_tiles=n_rope_tiles, k_scale=RET_DK ** -0.5)
    return pl.pallas_call(
        kern,
        grid=(t // tm, n // tn),
        in_specs=[
            pl.BlockSpec((tm, d), lambda i, j: (i, 0)),
            pl.BlockSpec((d, tn), lambda i, j: (0, j)),
            pl.BlockSpec((tm, RET_DK // 2), lambda i, j: (i % tiles_per_seq, 0)),
            pl.BlockSpec((tm, RET_DK // 2), lambda i, j: (i % tiles_per_seq, 0)),
        ],
        out_specs=pl.BlockSpec((tm, tn), lambda i, j: (i, j)),
        out_shape=jax.ShapeDtypeStruct((t, n), BF16),
        compiler_params=_cparams(("parallel", "arbitrary")),
        name="ret_proj",
    )(x2d, w_bf16, cos, sin)


def _ret_core_kernel(q_ref, k_ref, v_ref, g_ref, mask_ref, qd_ref, kd_ref, cd_ref, o_ref, state_ref):
    c = pl.program_id(2)

    @pl.when(c == 0)
    def _():
        state_ref[...] = jnp.zeros_like(state_ref)

    q = q_ref[...]
    k = k_ref[...]
    v = v_ref[...]
    inner = _nt_dot(q, k) * mask_ref[0]
    state = state_ref[...]
    qs = (q.astype(F32) * qd_ref[0]).astype(BF16)
    o = _dot(inner.astype(BF16), v) + _dot(qs, state.astype(BF16))
    ks = k.astype(F32) * kd_ref[0]
    state_ref[...] = state * cd_ref[0] + _dot(ks.T.astype(BF16), v)

    mu = jnp.mean(o, axis=-1, keepdims=True)
    var = jnp.mean(jnp.square(o - mu), axis=-1, keepdims=True)
    on = (o - mu) * lax.rsqrt(var + LN_EPS)
    g = g_ref[...].astype(F32)
    o_ref[...] = (g * jax.nn.sigmoid(g) * on).astype(o_ref.dtype)


def _ret_core(proj, batch, seq):
    t = proj.shape[0]
    h, dk, dv, c = RET_HEADS, RET_DK, RET_DV, min(RET_CHUNK, seq)
    nc = seq // c
    log_gamma = jnp.log1p(-(2.0 ** (-5.0 - jnp.arange(h, dtype=F32))))
    idx = jnp.arange(c, dtype=F32)
    diff = idx[:, None] - idx[None, :]
    mask = jnp.where(diff >= 0, jnp.exp(log_gamma[:, None, None] * jnp.maximum(diff, 0.0)), 0.0)
    qd = jnp.broadcast_to(jnp.exp(log_gamma[:, None] * (idx[None, :] + 1.0))[:, :, None], (h, c, dk))
    kd = jnp.broadcast_to(jnp.exp(log_gamma[:, None] * (c - 1.0 - idx[None, :]))[:, :, None], (h, c, dk))
    cd = jnp.broadcast_to(jnp.exp(log_gamma * c)[:, None, None], (h, 1, dv))
    v_off = 2 * h * dk // dv
    g_off = v_off + h
    return pl.pallas_call(
        _ret_core_kernel,
        grid=(batch, h, nc),
        in_specs=[
            pl.BlockSpec((c, dk), lambda b, hh, cc: (b * nc + cc, hh)),
            pl.BlockSpec((c, dk), lambda b, hh, cc: (b * nc + cc, h + hh)),
            pl.BlockSpec((c, dv), lambda b, hh, cc: (b * nc + cc, v_off + hh)),
            pl.BlockSpec((c, dv), lambda b, hh, cc: (b * nc + cc, g_off + hh)),
            pl.BlockSpec((1, c, c), lambda b, hh, cc: (hh, 0, 0)),
            pl.BlockSpec((1, c, dk), lambda b, hh, cc: (hh, 0, 0)),
            pl.BlockSpec((1, c, dk), lambda b, hh, cc: (hh, 0, 0)),
            pl.BlockSpec((1, 1, dv), lambda b, hh, cc: (hh, 0, 0)),
        ],
        out_specs=pl.BlockSpec((c, dv), lambda b, hh, cc: (b * nc + cc, hh)),
        out_shape=jax.ShapeDtypeStruct((t, h * dv), BF16),
        scratch_shapes=[pltpu.VMEM((dk, dv), F32)],
        compiler_params=_cparams(("parallel", "parallel", "arbitrary")),
        name="ret_core",
    )(proj, proj, proj, proj, mask, qd, kd, cd)


def _layer_norm_rows(y, g, b):
    mu = jnp.mean(y, axis=-1, keepdims=True)
    var = jnp.mean(jnp.square(y - mu), axis=-1, keepdims=True)
    return (y - mu) * lax.rsqrt(var + LN_EPS) * g + b


def _split_bf16(x):
    hi = x.astype(BF16)
    lo = (x - hi.astype(F32)).astype(BF16)
    return hi, lo


def _route(hn, rwt_hi_ref, rwt_lo_ref, rb_ref, cls_ref):
    h_hi, h_lo = _split_bf16(hn)
    w_hi = rwt_hi_ref[...]
    logits = _nt_dot(w_hi, h_hi) + _nt_dot(w_hi, h_lo) + _nt_dot(rwt_lo_ref[...], h_hi)
    aff = jax.nn.sigmoid(logits)
    sel = aff + rb_ref[...]
    tm = hn.shape[0]

    def row(a, e):
        return a[e:e + 1, :]

    best_g = jnp.zeros((1, tm), jnp.int32)
    best_s = None
    for gi in range(N_EXPERT_GROUPS):
        a0, a1, a2, a3 = [row(sel, EXPERTS_PER_GROUP * gi + e) for e in range(EXPERTS_PER_GROUP)]
        hi01, lo01 = jnp.maximum(a0, a1), jnp.minimum(a0, a1)
        hi23, lo23 = jnp.maximum(a2, a3), jnp.minimum(a2, a3)
        top2 = jnp.maximum(hi01, hi23) + jnp.maximum(jnp.minimum(hi01, hi23), jnp.maximum(lo01, lo23))
        if best_s is None:
            best_s = top2
        else:
            upd = top2 > best_s
            best_g = jnp.where(upd, gi, best_g)
            best_s = jnp.where(upd, top2, best_s)

    def pick(a, e):
        out = row(a, e)
        for gi in range(1, N_EXPERT_GROUPS):
            out = jnp.where(best_g == gi, row(a, EXPERTS_PER_GROUP * gi + e), out)
        return out

    cs = [pick(sel, e) for e in range(EXPERTS_PER_GROUP)]
    af = [pick(aff, e) for e in range(EXPERTS_PER_GROUP)]
    i1 = jnp.zeros((1, tm), jnp.int32)
    s1, a1v = cs[0], af[0]
    for e in range(1, EXPERTS_PER_GROUP):
        upd = cs[e] > s1
        i1 = jnp.where(upd, e, i1)
        s1 = jnp.where(upd, cs[e], s1)
        a1v = jnp.where(upd, af[e], a1v)
    i2 = jnp.full((1, tm), -1, jnp.int32)
    s2 = jnp.full((1, tm), -jnp.inf, F32)
    a2v = jnp.zeros((1, tm), F32)
    for e in range(EXPERTS_PER_GROUP):
        upd = (i1 != e) & ((i2 < 0) | (cs[e] > s2))
        i2 = jnp.where(upd, e, i2)
        s2 = jnp.where(upd, cs[e], s2)
        a2v = jnp.where(upd, af[e], a2v)
    tot = a1v + a2v
    w1, w2 = a1v / tot, a2v / tot
    first_is_lo = i1 < i2
    lo = jnp.where(first_is_lo, i1, i2)
    hi = jnp.where(first_is_lo, i2, i1)
    pair = jnp.right_shift(lo * (2 * EXPERTS_PER_GROUP - 1 - lo), 1) + (hi - lo - 1)
    cls = best_g * N_PAIRS + pair
    cls_ref[...] = jnp.concatenate([cls, jnp.zeros((7, tm), jnp.int32)], axis=0)
    w_lo = jnp.where(first_is_lo, w1, w2)
    w_hi_ = jnp.where(first_is_lo, w2, w1)
    return jnp.concatenate([w_lo, w_hi_, jnp.zeros((LANES - 2, tm), F32)], axis=0).T


def _mix_out_kernel(a_ref, w_ref, h_ref, g_ref, b_ref, rwt_hi_ref, rwt_lo_ref, rb_ref, o_ref, cls_ref):
    mix = _dot(a_ref[...], w_ref[...])
    hn = _layer_norm_rows(ALPHA * h_ref[...] + mix, g_ref[...], b_ref[...])
    d = hn.shape[1]
    o_ref[:, :d] = hn
    o_ref[:, d:] = _route(hn, rwt_hi_ref, rwt_lo_ref, rb_ref, cls_ref)


def _mix_out(a_bf16, w_bf16, h2d, ln_g, ln_b, rwt_hi, rwt_lo, rb):
    t = a_bf16.shape[0]
    d = h2d.shape[1]
    k = a_bf16.shape[1]
    tm = ROW_TILE
    row = lambda i: (i, 0)
    fixed = lambda i: (0, 0)
    return pl.pallas_call(
        _mix_out_kernel,
        grid=(t // tm,),
        in_specs=[
            pl.BlockSpec((tm, k), row),
            pl.BlockSpec((k, d), fixed),
            pl.BlockSpec((tm, d), row),
            pl.BlockSpec((1, d), fixed),
            pl.BlockSpec((1, d), fixed),
            pl.BlockSpec((N_EXPERTS, d), fixed),
            pl.BlockSpec((N_EXPERTS, d), fixed),
            pl.BlockSpec((N_EXPERTS, 1), fixed),
        ],
        out_specs=[
            pl.BlockSpec((tm, d + LANES), row),
            pl.BlockSpec((8, tm), lambda i: (0, i)),
        ],
        out_shape=[
            jax.ShapeDtypeStruct((t, d + LANES), F32),
            jax.ShapeDtypeStruct((8, t), jnp.int32),
        ],
        compiler_params=_cparams(("parallel",)),
        name="mix_out",
    )(a_bf16, w_bf16, h2d, ln_g, ln_b, rwt_hi, rwt_lo, rb)


def _expert_kernel(ea_ref, eb_ref, dst_ref, nused_ref, h_hbm, wga_ref, wua_ref, wda_ref, wgb_ref, wub_ref,
                   wdb_ref, g_ref, b_ref, o_hbm, xbuf, ybuf, gsem, ssem, *, n_tokens):
    i = pl.program_id(0)
    tile = xbuf.shape[1]
    d = ybuf.shape[2]
    n_used = nused_ref[0]

    def start_gather(step, slot):
        base = step * tile
        for r in range(tile):
            src = jnp.minimum(dst_ref[base + r], n_tokens - 1)
            pltpu.make_async_copy(h_hbm.at[pl.ds(src, 1)], xbuf.at[slot, pl.ds(r, 1)], gsem.at[slot]).start()

    def start_scatter(step, slot):
        base = step * tile
        for r in range(tile):
            dst = dst_ref[base + r]
            pltpu.make_async_copy(ybuf.at[slot, pl.ds(r, 1)], o_hbm.at[pl.ds(dst, 1)], ssem.at[slot]).start()

    def wait_gather(slot):
        pltpu.make_async_copy(h_hbm.at[pl.ds(0, tile)], xbuf.at[slot], gsem.at[slot]).wait()

    def wait_scatter(slot):
        pltpu.make_async_copy(ybuf.at[slot], o_hbm.at[pl.ds(0, tile)], ssem.at[slot]).wait()

    @pl.when(i == 0)
    def _():
        ybuf[...] = jnp.zeros_like(ybuf)
        for slot in range(2):
            fill = pltpu.make_async_copy(ybuf.at[slot], o_hbm.at[pl.ds(n_tokens + slot * tile, tile)], ssem.at[slot])
            fill.start()
            fill.wait()
        start_gather(i, 0)

    def step(slot):
        @pl.when(i + 1 < n_used)
        def _():
            start_gather(i + 1, 1 - slot)

        wait_gather(slot)

        @pl.when(i >= 2)
        def _():
            wait_scatter(slot)

        xe = xbuf[slot]
        x = xe[:, :d]
        xb = x.astype(BF16)

        def expert(wg_ref, wu_ref, wd_ref):
            gate = _dot(xb, wg_ref[0])
            up = _dot(xb, wu_ref[0])
            return _dot((gate * jax.nn.sigmoid(gate) * up).astype(BF16), wd_ref[0])

        y = xe[:, d:d + 1] * expert(wga_ref, wua_ref, wda_ref) + xe[:, d + 1:d + 2] * expert(wgb_ref, wub_ref, wdb_ref)
        ybuf[slot] = _layer_norm_rows(ALPHA * x + y, g_ref[...], b_ref[...])
        start_scatter(i, slot)

        @pl.when(i == n_used - 1)
        def _():
            @pl.when(i >= 1)
            def _():
                wait_scatter(1 - slot)
            wait_scatter(slot)

    @pl.when((i < n_used) & (i % 2 == 0))
    def _():
        step(0)

    @pl.when((i < n_used) & (i % 2 == 1))
    def _():
        step(1)


def _experts(h1x, tile_ea, tile_eb, row_dst, n_used, wg, wu, wd, ln_g, ln_b, n_tokens):
    d = wg.shape[1]
    de = wg.shape[2]
    n_tiles = tile_ea.shape[0]
    ea = lambda i, a, b, rd, nu: (a[i], 0, 0)
    eb = lambda i, a, b, rd, nu: (b[i], 0, 0)
    fixed = lambda i, a, b, rd, nu: (0, 0)
    grid_spec = pltpu.PrefetchScalarGridSpec(
        num_scalar_prefetch=4,
        grid=(n_tiles,),
        in_specs=[
            pl.BlockSpec(memory_space=pl.ANY),
            pl.BlockSpec((1, d, de), ea), pl.BlockSpec((1, d, de), ea), pl.BlockSpec((1, de, d), ea),
            pl.BlockSpec((1, d, de), eb), pl.BlockSpec((1, d, de), eb), pl.BlockSpec((1, de, d), eb),
            pl.BlockSpec((1, d), fixed), pl.BlockSpec((1, d), fixed),
        ],
        out_specs=pl.BlockSpec(memory_space=pl.ANY),
        scratch_shapes=[pltpu.VMEM((2, MOE_TILE, h1x.shape[1]), F32), pltpu.VMEM((2, MOE_TILE, d), F32),
                        pltpu.SemaphoreType.DMA((2,)), pltpu.SemaphoreType.DMA((2,))],
    )
    return pl.pallas_call(
        functools.partial(_expert_kernel, n_tokens=n_tokens),
        grid_spec=grid_spec,
        out_shape=jax.ShapeDtypeStruct((n_tokens + 2 * MOE_TILE, d), F32),
        compiler_params=_cparams(("arbitrary",)),
        name="moe_experts",
    )(tile_ea, tile_eb, row_dst, n_used, h1x, wg, wu, wd, wg, wu, wd, ln_g, ln_b)


def _moe_plan(cls):
    t = cls.shape[0]
    onehot = (cls[:, None] == jnp.arange(N_CLASSES)[None, :]).astype(jnp.int32)
    csum = jnp.cumsum(onehot, axis=0)
    rank = jnp.take_along_axis(csum, cls[:, None], axis=1)[:, 0] - 1
    counts = csum[-1]
    padded = (counts + MOE_TILE - 1) // MOE_TILE * MOE_TILE
    ends = jnp.cumsum(padded)
    pos = (ends - padded)[cls] + rank
    n_tiles = t // MOE_TILE + N_CLASSES
    row = jnp.arange(n_tiles * MOE_TILE, dtype=jnp.int32)
    padding_dst = t + (row // MOE_TILE % 2) * MOE_TILE + row % MOE_TILE
    row_dst = padding_dst.at[pos].set(jnp.arange(t, dtype=jnp.int32))
    tile_start = jnp.arange(n_tiles, dtype=jnp.int32) * MOE_TILE
    tile_cls = jnp.minimum(jnp.sum((tile_start[:, None] >= ends[None, :]).astype(jnp.int32), axis=1), N_CLASSES - 1)
    pair_lo = np.array([a for a in range(EXPERTS_PER_GROUP) for b in range(a + 1, EXPERTS_PER_GROUP)], np.int32)
    pair_hi = np.array([b for a in range(EXPERTS_PER_GROUP) for b in range(a + 1, EXPERTS_PER_GROUP)], np.int32)
    base = tile_cls // N_PAIRS * EXPERTS_PER_GROUP
    tile_ea = (base + jnp.asarray(pair_lo)[tile_cls % N_PAIRS]).astype(jnp.int32)
    tile_eb = (base + jnp.asarray(pair_hi)[tile_cls % N_PAIRS]).astype(jnp.int32)
    n_used = (ends[-1] // MOE_TILE).astype(jnp.int32).reshape(1)
    return tile_ea, tile_eb, row_dst, n_used


def _nsa_proj_kernel(x_ref, w_ref, cos_ref, sin_ref, o_ref, *, n_kv_tiles):
    j = pl.program_id(1)
    acc = _dot(x_ref[...].astype(BF16), w_ref[...])
    tn = acc.shape[1]
    roped = (j >= n_kv_tiles) | (j % 2 == 0)

    @pl.when(roped)
    def _():
        for hh in range(tn // HEAD_PITCH):
            lanes = slice(hh * HEAD_PITCH, (hh + 1) * HEAD_PITCH)
            a = acc[:, lanes]
            swapped = pltpu.roll(a, HEAD_PITCH // 2, axis=1)
            o_ref[:, lanes] = (a * cos_ref[0, :, lanes] + swapped * sin_ref[0, :, lanes]).astype(o_ref.dtype)

    @pl.when(jnp.logical_not(roped))
    def _():
        lane = lax.broadcasted_iota(jnp.int32, (1, tn), 1)
        o_ref[...] = (acc + jnp.where(lane % HEAD_PITCH == NSA_DH, 1.0, 0.0)).astype(o_ref.dtype)


def _nsa_proj(x2d, w_bf16, seq, t):
    d = x2d.shape[1]
    n = w_bf16.shape[1]
    tm, tn = ROW_TILE, NSA_GROUPS * HEAD_PITCH
    n_kv_tiles = 6
    k_tabs = _rope_tables_split(seq, NSA_DH, HEAD_PITCH, tn, 1.0)
    q_tabs = _rope_tables_split(seq, NSA_DH, HEAD_PITCH, tn, NSA_DH ** -0.5 * math.log2(math.e))
    cos_t = jnp.stack([k_tabs[0], q_tabs[0]])
    sin_t = jnp.stack([k_tabs[1], q_tabs[1]])
    tiles_per_seq = seq // tm
    tab = lambda i, j: (j // n_kv_tiles, i % tiles_per_seq, 0)
    return pl.pallas_call(
        functools.partial(_nsa_proj_kernel, n_kv_tiles=n_kv_tiles),
        grid=(t // tm, n // tn),
        in_specs=[
            pl.BlockSpec((tm, d), lambda i, j: (i, 0)),
            pl.BlockSpec((d, tn), lambda i, j: (0, j)),
            pl.BlockSpec((1, tm, tn), tab),
            pl.BlockSpec((1, tm, tn), tab),
        ],
        out_specs=pl.BlockSpec((tm, tn), lambda i, j: (i, j)),
        out_shape=jax.ShapeDtypeStruct((t, n), BF16),
        compiler_params=_cparams(("parallel", "arbitrary")),
        name="nsa_proj",
    )(x2d, w_bf16, cos_t, sin_t)


def _gate_kernel(x_ref, w_ref, o_ref):
    o_ref[...] = jax.nn.sigmoid(_dot(x_ref[...].astype(BF16), w_ref[...]))


def _nsa_gates(x2d, wg_bf16, t):
    d = x2d.shape[1]
    tm = ROW_TILE
    return pl.pallas_call(
        _gate_kernel,
        grid=(t // tm,),
        in_specs=[pl.BlockSpec((tm, d), lambda i: (i, 0)), pl.BlockSpec((d, LANES), lambda i: (0, 0))],
        out_specs=pl.BlockSpec((tm, LANES), lambda i: (i, 0)),
        out_shape=jax.ShapeDtypeStruct((t, LANES), F32),
        compiler_params=_cparams(("parallel",)),
        name="nsa_gates",
    )(x2d, wg_bf16)


def _gelu_tanh(x):
    return 0.5 * x * (1.0 + jnp.tanh(math.sqrt(2.0 / math.pi) * (x + 0.044715 * (x * x * x))))


def _cmp_kernel(x_ref, pe_ref, w1_ref, w2_ref, o_ref):
    x = x_ref[0, 0, 0].astype(F32)
    n = x.shape[0]
    xa = (x + pe_ref[0, 0:1, :]).astype(BF16)
    xb = (x + pe_ref[0, 1:2, :]).astype(BF16)
    u = _dot(xa, w1_ref[0, 0])
    v = _dot(xb, w1_ref[0, 1])
    pre = u + pltpu.roll(v, n - 1, axis=0)
    out = _dot(_gelu_tanh(pre).astype(BF16), w2_ref[0])
    lane = lax.broadcasted_iota(jnp.int32, (1, out.shape[1]), 1)
    ones_lane = jnp.where((lane == NSA_DH) & (pl.program_id(0) == 1), 1.0, 0.0)
    o_ref[0, 0, 0] = (out + ones_lane).astype(o_ref.dtype)


def _compress(xkv, pe, w1, w2):
    _, b, g, n, w = xkv.shape
    pitch = w2.shape[2]
    return pl.pallas_call(
        _cmp_kernel,
        grid=(2, b, g),
        in_specs=[
            pl.BlockSpec((1, 1, 1, n, w), lambda s, bb, gg: (s, bb, gg, 0, 0)),
            pl.BlockSpec((1, 2, w), lambda s, bb, gg: (s, 0, 0)),
            pl.BlockSpec((1, 2, w, CMP_HIDDEN), lambda s, bb, gg: (s, 0, 0, 0)),
            pl.BlockSpec((1, CMP_HIDDEN, pitch), lambda s, bb, gg: (s, 0, 0)),
        ],
        out_specs=pl.BlockSpec((1, 1, 1, n, pitch), lambda s, bb, gg: (s, bb, gg, 0, 0)),
        out_shape=jax.ShapeDtypeStruct((2, b, g, n, pitch), BF16),
        compiler_params=_cparams(("parallel", "parallel", "parallel")),
        name="nsa_compress",
    )(xkv, pe, w1, w2)


def _softmax_pv(s_t, v_t):
    m = jnp.max(s_t, axis=0, keepdims=True)
    p = jnp.exp2(s_t - m).astype(BF16)
    return p, _dot(v_t, p)


def _normalize(ov_t):
    return ov_t * (1.0 / ov_t[NSA_DH:NSA_DH + 1, :])


def _nsa_attn_kernel(q_ref, gate_ref, kc_ref, vc_ref, ks_ref, vs_ref, kw_ref, vw_ref, ovl_ref, xt_ref,
                     o_ref, m_ref, acc_ref, s0_ref, s1_ref, *, seq, top_k):
    i = pl.program_id(2)
    s_refs = (s0_ref, s1_ref)
    qb, rep, pitch = Q_BLOCK, NSA_REP, HEAD_PITCH
    t0 = i * qb
    q2 = q_ref[...]
    q4 = jnp.concatenate([q2[:, r * pitch:(r + 1) * pitch] for r in range(rep)], axis=0)
    t_lane = t0 + lax.broadcasted_iota(jnp.int32, (1, qb), 1)
    t_lane4 = jnp.concatenate([t_lane] * rep, axis=1)

    def add_cols(s_t, bias_t):
        return jnp.concatenate([s_t[:, r * qb:(r + 1) * qb] + bias_t for r in range(rep)], axis=1)

    n_cmp_pad = kc_ref.shape[2]
    cmp_end = lax.broadcasted_iota(jnp.int32, (n_cmp_pad, qb), 0) * CMP_STRIDE + (CMP_LEN - 1)
    s_c = add_cols(_nt_dot(kc_ref[0, 0], q4), jnp.where(cmp_end <= t_lane, 0.0, NEG))
    p_c, ov_c = _softmax_pv(s_c, vc_ref[0, 0, :V_ROWS])
    o_cmp = jnp.where(t_lane4 >= CMP_LEN - 1, _normalize(ov_c), 0.0)

    ovl = ovl_ref[...]
    nsel = ovl.shape[0] - 8
    imp = jnp.zeros((nsel, qb), F32)
    for r in range(rep):
        ext = _dot(ovl, p_c[:, r * qb:(r + 1) * qb])
        imp = imp + ext[:nsel] * (1.0 / ext[nsel:nsel + 1])

    def values_t(v_ref, first_tile, n_tiles):
        return jnp.concatenate([v_ref[0, 0, first_tile + u, :V_ROWS] for u in range(n_tiles)], axis=1)

    wk = min(seq, qb + WINDOW)
    w_start = pl.multiple_of(jnp.maximum(t0 + qb - wk, 0), qb)
    dist = t_lane - (w_start + lax.broadcasted_iota(jnp.int32, (wk, qb), 0))
    bias_w = jnp.where((dist >= 0) & (dist < WINDOW), 0.0, NEG)
    s_w = add_cols(_nt_dot(kw_ref[pl.ds(w_start, wk), :], q4), bias_w)
    _, ov_w = _softmax_pv(s_w, values_t(vw_ref, w_start // LANES, wk // LANES))
    o_win = _normalize(ov_w)

    blk = lax.broadcasted_iota(jnp.int32, (nsel, qb), 0)
    t_col = t0 + lax.broadcasted_iota(jnp.int32, (nsel, qb), 1)
    cur = t_col // SLC_BLOCK
    valid = (blk * SLC_BLOCK <= t_col) & (blk < seq // SLC_BLOCK)
    forced = (blk == 0) | (blk == cur) | (blk == cur - 1)
    score = jnp.where(valid, imp + jnp.where(forced, FORCE_BONUS, 0.0), NEG)
    blk_f = blk.astype(F32)
    chosen = jnp.zeros((nsel, qb), F32)
    for _ in range(top_k):
        m = jnp.max(score, axis=0, keepdims=True)
        first = jnp.min(jnp.where(score == m, blk_f, float(nsel)), axis=0, keepdims=True)
        hit = blk_f == first
        chosen = jnp.where(hit, 1.0, chosen)
        score = jnp.where(hit, -jnp.inf, score)
    chosen = jnp.where(valid, chosen, 0.0)
    sel_bias = ((chosen.T - 1.0) * -NEG).astype(BF16)

    q_aug = jnp.concatenate([q4, jnp.concatenate([sel_bias] * rep, axis=0)], axis=1)
    kc_len = min(SLC_CHUNK, seq)
    last = t0 // kc_len
    k_row = lax.broadcasted_iota(jnp.int32, (kc_len, qb), 0)
    q_lane = lax.broadcasted_iota(jnp.int32, (kc_len, qb), 1) + (t0 - last * kc_len)
    causal = jnp.where(k_row <= q_lane, 0.0, NEG)

    def slc_scores(c, slot):
        start = pl.multiple_of(c * kc_len, kc_len)
        k_aug = jnp.concatenate([ks_ref[pl.ds(start, kc_len), :], xt_ref[pl.ds(start, kc_len), :]], axis=1)
        s = _nt_dot(k_aug, q_aug)
        s_refs[slot][...] = add_cols(s, causal * (c == last).astype(F32))

    def slc_update(c, slot, m_old, acc_old):
        s = s_refs[slot][...]
        m_new = jnp.maximum(m_old, jnp.max(s, axis=0, keepdims=True))
        p = jnp.exp2(s - m_new).astype(BF16)
        v_t = values_t(vs_ref, c * (kc_len // LANES), kc_len // LANES)
        return m_new, jnp.exp2(m_old - m_new) * acc_old + _dot(v_t, p)

    def chunk_pair(j, carry):
        slc_scores(2 * j + 1, 1)
        m, acc = slc_update(2 * j, 0, m_ref[...], acc_ref[...])
        slc_scores(jnp.minimum(2 * j + 2, last), 0)
        m, acc = slc_update(2 * j + 1, 1, m, acc)
        m_ref[...] = m
        acc_ref[...] = acc
        return carry

    m_ref[...] = jnp.full(m_ref.shape, NEG, F32)
    acc_ref[...] = jnp.zeros_like(acc_ref)
    slc_scores(0, 0)
    lax.fori_loop(0, (last + 1) // 2, chunk_pair, 0)

    @pl.when(last % 2 == 0)
    def _():
        m, acc = slc_update(last, 0, m_ref[...], acc_ref[...])
        m_ref[...] = m
        acc_ref[...] = acc

    o_slc = _normalize(acc_ref[...])

    gates = gate_ref[0, 0]
    head_lane = lax.broadcasted_iota(jnp.int32, (V_ROWS, qb), 0) < NSA_DH
    no_lanes = jnp.zeros((pitch - V_ROWS, qb), F32)
    outs = []
    for r in range(rep):
        cols = slice(r * qb, (r + 1) * qb)
        o_r = (gates[3 * r:3 * r + 1] * o_cmp[:, cols] + gates[3 * r + 1:3 * r + 2] * o_slc[:, cols]
               + gates[3 * r + 2:3 * r + 3] * o_win[:, cols])
        outs.append(jnp.concatenate([jnp.where(head_lane, o_r, 0.0), no_lanes], axis=0).T)
    o_ref[...] = jnp.concatenate(outs, axis=1).astype(o_ref.dtype)


def _nsa_attention(kvq, gates_t, k_cmp, v_cmp_t, vs_t, vw_t, batch, seq):
    t = kvq.shape[0]
    g, rep, pitch = NSA_GROUPS, NSA_REP, HEAD_PITCH
    nq = seq // Q_BLOCK
    n_cmp_pad = k_cmp.shape[2]
    nsel = seq // SLC_BLOCK
    nsel_pad = max(LANES, nsel)
    ci = np.arange(n_cmp_pad)[None, :]
    sj = np.arange(nsel_pad)[:, None]
    overlap_t = ((ci * CMP_STRIDE < (sj + 1) * SLC_BLOCK) & (ci * CMP_STRIDE + CMP_LEN > sj * SLC_BLOCK)
                 & (ci < n_cmp_pad - 1) & (sj < nsel))
    ovl = jnp.asarray(np.concatenate([overlap_t, np.ones((8, n_cmp_pad), bool)]), BF16)
    kc_len = min(SLC_CHUNK, seq)
    key_block = jnp.asarray((np.arange(seq) // SLC_BLOCK)[:, None] == np.arange(nsel_pad)[None, :], BF16)
    kern = functools.partial(_nsa_attn_kernel, seq=seq, top_k=min(SLC_TOPK, nsel))
    per_bg = lambda b, gg, i: (b, gg, 0, 0)
    fixed = lambda b, gg, i: (0, 0)

    def keys(idx):
        return pl.BlockSpec((seq, pitch), lambda b, gg, i: (b, idx * g + gg))

    values = pl.BlockSpec((1, 1, seq // LANES, pitch, LANES), lambda b, gg, i: (b, gg, 0, 0, 0))
    return pl.pallas_call(
        kern,
        grid=(batch, g, nq),
        in_specs=[
            pl.BlockSpec((Q_BLOCK, rep * pitch), lambda b, gg, i: (b * nq + i, 6 + gg)),
            pl.BlockSpec((1, 1, 3 * rep, Q_BLOCK), lambda b, gg, i: (b, gg, 0, i)),
            pl.BlockSpec((1, 1, n_cmp_pad, pitch), per_bg),
            pl.BlockSpec((1, 1, pitch, n_cmp_pad), per_bg),
            keys(2), values, keys(4), values,
            pl.BlockSpec(ovl.shape, fixed),
            pl.BlockSpec(key_block.shape, fixed),
        ],
        out_specs=pl.BlockSpec((Q_BLOCK, rep * pitch), lambda b, gg, i: (b * nq + i, gg)),
        out_shape=jax.ShapeDtypeStruct((t, NSA_HEADS * pitch), BF16),
        scratch_shapes=[pltpu.VMEM((1, rep * Q_BLOCK), F32), pltpu.VMEM((V_ROWS, rep * Q_BLOCK), F32),
                        pltpu.VMEM((kc_len, rep * Q_BLOCK), F32), pltpu.VMEM((kc_len, rep * Q_BLOCK), F32)],
        compiler_params=_cparams(("parallel", "parallel", "arbitrary")),
        name="nsa_attention",
    )(kvq, gates_t, k_cmp, v_cmp_t, kvq, vs_t, kvq, vw_t, ovl, key_block)


def _moe_layer(h1x, cls, wg, wu, wd, ln_g, ln_b):
    tile_ea, tile_eb, row_dst, n_used = _moe_plan(cls[0])
    return _experts(h1x, tile_ea, tile_eb, row_dst, n_used, wg, wu, wd, ln_g, ln_b, h1x.shape[0])


def kernel(x, ret_w_in, ret_w_out, nsa_w_kv, cmp_pe_k, cmp_pe_v, cmp_k_w1, cmp_k_w2, cmp_v_w1, cmp_v_w2,
           nsa_w_in, nsa_w_out, router_w, router_b, moe_w_gate, moe_w_up, moe_w_down,
           ln_mix_g, ln_mix_b, ln_ffn_g, ln_ffn_b):
    batch, seq, d = x.shape
    t = batch * seq
    h = x.reshape(t, d)

    rwt = router_w.T
    rwt_hi = rwt.astype(BF16)
    rwt_lo = (rwt - rwt_hi.astype(F32)).astype(BF16)
    rb = router_b.reshape(N_EXPERTS, 1).astype(F32)
    row = lambda v: v.reshape(1, d)

    proj = _ret_proj(h, ret_w_in[0].astype(BF16), seq)
    gated = _ret_core(proj, batch, seq)
    h1x, cls = _mix_out(gated, ret_w_out[0].astype(BF16), h, row(ln_mix_g[0]), row(ln_mix_b[0]),
                        rwt_hi, rwt_lo, rb)
    h = _moe_layer(h1x, cls, moe_w_gate[0].astype(BF16), moe_w_up[0].astype(BF16),
                   moe_w_down[0].astype(BF16), row(ln_ffn_g[0]), row(ln_ffn_b[0]))

    g, dh, rep, pitch = NSA_GROUPS, NSA_DH, NSA_REP, HEAD_PITCH
    n_q = NSA_HEADS * dh

    def pitch_cols(w):
        heads = w.shape[-1] // dh
        lead = w.shape[:-1]
        padded = jnp.pad(w.reshape(lead + (heads, dh)), [(0, 0)] * (len(lead) + 1) + [(0, pitch - dh)])
        return padded.reshape(lead + (heads * pitch,))

    kv_parts = [nsa_w_kv[:, p * KV_W:(p + 1) * KV_W] for p in range(6)]
    w_all = jnp.concatenate(
        [_split_halves(w, dh, pitch) if p % 2 == 0 else pitch_cols(w) for p, w in enumerate(kv_parts)]
        + [_split_halves(nsa_w_in[0][:, :n_q], dh, pitch)], axis=1).astype(BF16)
    kvq = _nsa_proj(h, w_all, seq, t)
    w_gate = jnp.pad(nsa_w_in[0][:, n_q:], ((0, 0), (0, LANES - 3 * NSA_HEADS))).astype(BF16)
    gates = _nsa_gates(h, w_gate, t)[:, :3 * NSA_HEADS]
    gates_t = gates.reshape(batch, seq, g, 3 * rep).transpose(0, 2, 3, 1)

    def blocks16(p, split):
        p = p.reshape(batch, seq // CMP_STRIDE, CMP_STRIDE, g, pitch)
        if split:
            p = jnp.concatenate([p[..., :dh // 2], p[..., pitch // 2:pitch // 2 + dh // 2]], axis=-1)
        else:
            p = p[..., :dh]
        return p.transpose(0, 3, 1, 2, 4).reshape(batch, g, seq // CMP_STRIDE, CMP_STRIDE * dh)

    gw = g * pitch
    xkv = jnp.stack([blocks16(kvq[:, :gw], True), blocks16(kvq[:, gw:2 * gw], False)])
    half_w = CMP_STRIDE * dh
    pe = jnp.stack([cmp_pe_k.reshape(2, half_w), cmp_pe_v.reshape(2, half_w)]).astype(F32)
    w1 = jnp.stack([cmp_k_w1.reshape(2, half_w, CMP_HIDDEN), cmp_v_w1.reshape(2, half_w, CMP_HIDDEN)]).astype(BF16)
    w2 = jnp.stack([_split_halves(cmp_k_w2, dh, pitch), pitch_cols(cmp_v_w2)]).astype(BF16)
    kv_cmp = _compress(xkv, pe, w1, w2)

    def keys_on_lanes(p):
        return p.reshape(batch, seq // LANES, LANES, g, pitch).transpose(0, 3, 1, 4, 2)

    attn = _nsa_attention(kvq, gates_t, kv_cmp[0], kv_cmp[1].transpose(0, 1, 3, 2),
                          keys_on_lanes(kvq[:, 3 * gw:4 * gw]), keys_on_lanes(kvq[:, 5 * gw:6 * gw]),
                          batch, seq)
    w_out = jnp.pad(nsa_w_out[0].reshape(NSA_HEADS, dh, d), ((0, 0), (0, pitch - dh), (0, 0))).reshape(
        NSA_HEADS * pitch, d).astype(BF16)
    h1x, cls = _mix_out(attn, w_out, h, row(ln_mix_g[1]), row(ln_mix_b[1]), rwt_hi, rwt_lo, rb)
    h = _moe_layer(h1x, cls, moe_w_gate[1].astype(BF16), moe_w_up[1].astype(BF16),
                   moe_w_down[1].astype(BF16), row(ln_ffn_g[1]), row(ln_ffn_b[1]))
    return h[:t].reshape(batch, seq, d)
```

```python
import functools
import math

import jax
import jax.numpy as jnp
import numpy as np
from jax import lax
from jax.experimental import pallas as pl
from jax.experimental.pallas import tpu as pltpu

F32 = jnp.float32
BF16 = jnp.bfloat16

D_MODEL = 1024
DEPTH = 2
ALPHA = (2.0 * DEPTH) ** 0.25
LN_EPS = 1e-5
ROPE_THETA = 10000.0
NEG = -1e30

RET_HEADS = 4
RET_DK = D_MODEL // RET_HEADS
RET_DV = 2 * RET_DK
RET_CHUNK = 256

NSA_HEADS = 16
NSA_GROUPS = 4
NSA_REP = NSA_HEADS // NSA_GROUPS
NSA_DH = D_MODEL // NSA_HEADS
KV_W = NSA_GROUPS * NSA_DH
CMP_LEN = 32
CMP_STRIDE = 16
CMP_HIDDEN = 256
SLC_BLOCK = 64
SLC_TOPK = 16
WINDOW = 512
Q_BLOCK = 256
FORCE_BONUS = 1e3
SLC_CHUNK = 512
SLC_SUBCHUNK = 256

N_EXPERTS = 16
N_EXPERT_GROUPS = 4
EXPERTS_PER_GROUP = N_EXPERTS // N_EXPERT_GROUPS
N_PAIRS = EXPERTS_PER_GROUP * (EXPERTS_PER_GROUP - 1) // 2
N_CLASSES = N_EXPERT_GROUPS * N_PAIRS
D_EXPERT = 512

LANES = 128
HEAD_PITCH = LANES
V_ROWS = NSA_DH + 16
ROW_TILE = 512
PROJ_ROW_TILE = 1024
MOE_TILE = 256
VMEM_LIMIT = 56 * 1024 * 1024


def _cparams(sem):
    return pltpu.CompilerParams(dimension_semantics=sem, vmem_limit_bytes=VMEM_LIMIT)


def _nt_dot(a, b):
    return lax.dot_general(a, b, (((1,), (1,)), ((), ())), preferred_element_type=F32)


def _dot(a, b):
    return jnp.dot(a, b, preferred_element_type=F32)


def _rope_tables_half(seq, d):
    inv = ROPE_THETA ** (-jnp.arange(0, d, 2, dtype=F32) / d)
    ang = jnp.arange(seq, dtype=F32)[:, None] * inv[None, :]
    return jnp.cos(ang), jnp.sin(ang)


def _split_halves(a, d, pitch):
    lead = a.shape[:-1]
    heads = a.shape[-1] // d
    a = a.reshape(lead + (heads, 2, d // 2))
    a = jnp.pad(a, [(0, 0)] * (len(lead) + 2) + [(0, (pitch - d) // 2)])
    return a.reshape(lead + (heads * pitch,))


def _rope_tables_split(seq, d, pitch, width, scale):
    cos, sin = _rope_tables_half(seq, d)
    reps = width // pitch
    cos_t = jnp.tile(_split_halves(jnp.concatenate([cos, cos], -1), d, pitch), (1, reps))
    sin_t = jnp.tile(_split_halves(jnp.concatenate([-sin, sin], -1), d, pitch), (1, reps))
    return cos_t * scale, sin_t * scale


def _ret_proj_kernel(x_ref, w_ref, cos_ref, sin_ref, o_ref, xb_ref, *, n_rope_tiles, k_scale):
    j = pl.program_id(1)

    @pl.when(j == 0)
    def _():
        xb_ref[...] = x_ref[...].astype(BF16)

    acc = _dot(xb_ref[...], w_ref[...])
    tn = acc.shape[1]
    half = RET_DK // 2

    @pl.when(j < n_rope_tiles)
    def _():
        c = cos_ref[...]
        s = sin_ref[...]
        scale = jnp.where(j >= n_rope_tiles // 2, k_scale, 1.0).astype(F32)
        for hh in range(tn // RET_DK):
            lo = hh * RET_DK
            x1 = acc[:, lo:lo + half]
            x2 = acc[:, lo + half:lo + RET_DK]
            o_ref[:, lo:lo + half] = ((x1 * c - x2 * s) * scale).astype(o_ref.dtype)
            o_ref[:, lo + half:lo + RET_DK] = ((x2 * c + x1 * s) * scale).astype(o_ref.dtype)

    @pl.when(j >= n_rope_tiles)
    def _():
        o_ref[...] = acc.astype(o_ref.dtype)


def _ret_proj(x2d, w_bf16, seq):
    t, d = x2d.shape
    n = w_bf16.shape[1]
    tm, tn = min(PROJ_ROW_TILE, seq), 1024
    cos, sin = _rope_tables_half(seq, RET_DK)
    tiles_per_seq = seq // tm
    n_rope_tiles = 2 * RET_HEADS * RET_DK // tn
    kern = functools.partial(_ret_proj_kernel, n_rope_tiles=n_rope_tiles, k_scale=RET_DK ** -0.5)
    return pl.pallas_call(
        kern,
        grid=(t // tm, n // tn),
        in_specs=[
            pl.BlockSpec((tm, d), lambda i, j: (i, 0)),
            pl.BlockSpec((d, tn), lambda i, j: (0, j)),
            pl.BlockSpec((tm, RET_DK // 2), lambda i, j: (i % tiles_per_seq, 0)),
            pl.BlockSpec((tm, RET_DK // 2), lambda i, j: (i % tiles_per_seq, 0)),
        ],
        out_specs=pl.BlockSpec((tm, tn), lambda i, j: (i, j)),
        out_shape=jax.ShapeDtypeStruct((t, n), BF16),
        scratch_shapes=[pltpu.VMEM((tm, d), BF16)],
        compiler_params=_cparams(("parallel", "arbitrary")),
        name="ret_proj",
    )(x2d, w_bf16, cos, sin)


def _ret_core_kernel(q_ref, k_ref, v_ref, g_ref, mask_ref, qd_ref, kd_ref, cd_ref, o_ref, state_ref):
    c = pl.program_id(2)

    @pl.when(c == 0)
    def _():
        state_ref[...] = jnp.zeros_like(state_ref)

    q = q_ref[...]
    k = k_ref[...]
    v = v_ref[...]
    inner = _nt_dot(q, k) * mask_ref[0]
    state = state_ref[...]
    qs = (q.astype(F32) * qd_ref[0]).astype(BF16)
    o = _dot(inner.astype(BF16), v) + _dot(qs, state.astype(BF16))
    ks = k.astype(F32) * kd_ref[0]
    state_ref[...] = state * cd_ref[0] + _dot(ks.T.astype(BF16), v)

    mu = jnp.mean(o, axis=-1, keepdims=True)
    var = jnp.mean(jnp.square(o - mu), axis=-1, keepdims=True)
    on = (o - mu) * lax.rsqrt(var + LN_EPS)
    g = g_ref[...].astype(F32)
    o_ref[...] = (g * jax.nn.sigmoid(g) * on).astype(o_ref.dtype)


def _ret_core(proj, batch, seq):
    t = proj.shape[0]
    h, dk, dv, c = RET_HEADS, RET_DK, RET_DV, min(RET_CHUNK, seq)
    nc = seq // c
    log_gamma = jnp.log1p(-(2.0 ** (-5.0 - jnp.arange(h, dtype=F32))))
    idx = jnp.arange(c, dtype=F32)
    diff = idx[:, None] - idx[None, :]
    mask = jnp.where(diff >= 0, jnp.exp(log_gamma[:, None, None] * jnp.maximum(diff, 0.0)), 0.0)
    qd = jnp.broadcast_to(jnp.exp(log_gamma[:, None] * (idx[None, :] + 1.0))[:, :, None], (h, c, dk))
    kd = jnp.broadcast_to(jnp.exp(log_gamma[:, None] * (c - 1.0 - idx[None, :]))[:, :, None], (h, c, dk))
    cd = jnp.broadcast_to(jnp.exp(log_gamma * c)[:, None, None], (h, 1, dv))
    v_off = 2 * h * dk // dv
    g_off = v_off + h
    return pl.pallas_call(
        _ret_core_kernel,
        grid=(batch, h, nc),
        in_specs=[
            pl.BlockSpec((c, dk), lambda b, hh, cc: (b * nc + cc, hh)),
            pl.BlockSpec((c, dk), lambda b, hh, cc: (b * nc + cc, h + hh)),
            pl.BlockSpec((c, dv), lambda b, hh, cc: (b * nc + cc, v_off + hh)),
            pl.BlockSpec((c, dv), lambda b, hh, cc: (b * nc + cc, g_off + hh)),
            pl.BlockSpec((1, c, c), lambda b, hh, cc: (hh, 0, 0)),
            pl.BlockSpec((1, c, dk), lambda b, hh, cc: (hh, 0, 0)),
            pl.BlockSpec((1, c, dk), lambda b, hh, cc: (hh, 0, 0)),
            pl.BlockSpec((1, 1, dv), lambda b, hh, cc: (hh, 0, 0)),
        ],
        out_specs=pl.BlockSpec((c, dv), lambda b, hh, cc: (b * nc + cc, hh)),
        out_shape=jax.ShapeDtypeStruct((t, h * dv), BF16),
        scratch_shapes=[pltpu.VMEM((dk, dv), F32)],
        compiler_params=_cparams(("parallel", "parallel", "arbitrary")),
        name="ret_core",
    )(proj, proj, proj, proj, mask, qd, kd, cd)


def _layer_norm_rows(y, g, b):
    mu = jnp.mean(y, axis=-1, keepdims=True)
    var = jnp.mean(jnp.square(y - mu), axis=-1, keepdims=True)
    return (y - mu) * lax.rsqrt(var + LN_EPS) * g + b


def _split_bf16(x):
    hi = x.astype(BF16)
    lo = (x - hi.astype(F32)).astype(BF16)
    return hi, lo


def _route(hn, rwt_hi_ref, rwt_lo_ref, rb_ref, cls_ref):
    h_hi, h_lo = _split_bf16(hn)
    w_hi = rwt_hi_ref[...]
    logits = _nt_dot(w_hi, h_hi) + _nt_dot(w_hi, h_lo) + _nt_dot(rwt_lo_ref[...], h_hi)
    aff = jax.nn.sigmoid(logits)
    sel = aff + rb_ref[...]
    tm = hn.shape[0]

    def row(a, e):
        return a[e:e + 1, :]

    best_g = jnp.zeros((1, tm), jnp.int32)
    best_s = None
    for gi in range(N_EXPERT_GROUPS):
        a0, a1, a2, a3 = [row(sel, EXPERTS_PER_GROUP * gi + e) for e in range(EXPERTS_PER_GROUP)]
        hi01, lo01 = jnp.maximum(a0, a1), jnp.minimum(a0, a1)
        hi23, lo23 = jnp.maximum(a2, a3), jnp.minimum(a2, a3)
        top2 = jnp.maximum(hi01, hi23) + jnp.maximum(jnp.minimum(hi01, hi23), jnp.maximum(lo01, lo23))
        if best_s is None:
            best_s = top2
        else:
            upd = top2 > best_s
            best_g = jnp.where(upd, gi, best_g)
            best_s = jnp.where(upd, top2, best_s)

    def pick(a, e):
        out = row(a, e)
        for gi in range(1, N_EXPERT_GROUPS):
            out = jnp.where(best_g == gi, row(a, EXPERTS_PER_GROUP * gi + e), out)
        return out

    cs = [pick(sel, e) for e in range(EXPERTS_PER_GROUP)]
    af = [pick(aff, e) for e in range(EXPERTS_PER_GROUP)]
    i1 = jnp.zeros((1, tm), jnp.int32)
    s1, a1v = cs[0], af[0]
    for e in range(1, EXPERTS_PER_GROUP):
        upd = cs[e] > s1
        i1 = jnp.where(upd, e, i1)
        s1 = jnp.where(upd, cs[e], s1)
        a1v = jnp.where(upd, af[e], a1v)
    i2 = jnp.full((1, tm), -1, jnp.int32)
    s2 = jnp.full((1, tm), -jnp.inf, F32)
    a2v = jnp.zeros((1, tm), F32)
    for e in range(EXPERTS_PER_GROUP):
        upd = (i1 != e) & ((i2 < 0) | (cs[e] > s2))
        i2 = jnp.where(upd, e, i2)
        s2 = jnp.where(upd, cs[e], s2)
        a2v = jnp.where(upd, af[e], a2v)
    tot = a1v + a2v
    w1, w2 = a1v / tot, a2v / tot
    first_is_lo = i1 < i2
    lo = jnp.where(first_is_lo, i1, i2)
    hi = jnp.where(first_is_lo, i2, i1)
    pair = jnp.right_shift(lo * (2 * EXPERTS_PER_GROUP - 1 - lo), 1) + (hi - lo - 1)
    cls = best_g * N_PAIRS + pair
    cls_ref[...] = jnp.concatenate([cls, jnp.zeros((7, tm), jnp.int32)], axis=0)
    w_lo = jnp.where(first_is_lo, w1, w2)
    w_hi_ = jnp.where(first_is_lo, w2, w1)
    return jnp.concatenate([w_lo, w_hi_, jnp.zeros((LANES - 2, tm), F32)], axis=0).T


def _mix_out_kernel(a_ref, w_ref, h_ref, g_ref, b_ref, rwt_hi_ref, rwt_lo_ref, rb_ref, o_ref, cls_ref):
    mix = _dot(a_ref[...], w_ref[...])
    hn = _layer_norm_rows(ALPHA * h_ref[...] + mix, g_ref[...], b_ref[...])
    d = hn.shape[1]
    o_ref[:, :d] = hn
    o_ref[:, d:] = _route(hn, rwt_hi_ref, rwt_lo_ref, rb_ref, cls_ref)


def _mix_out(a_bf16, w_bf16, h2d, ln_g, ln_b, rwt_hi, rwt_lo, rb):
    t = a_bf16.shape[0]
    d = h2d.shape[1]
    k = a_bf16.shape[1]
    tm = ROW_TILE
    row = lambda i: (i, 0)
    fixed = lambda i: (0, 0)
    return pl.pallas_call(
        _mix_out_kernel,
        grid=(t // tm,),
        in_specs=[
            pl.BlockSpec((tm, k), row),
            pl.BlockSpec((k, d), fixed),
            pl.BlockSpec((tm, d), row),
            pl.BlockSpec((1, d), fixed),
            pl.BlockSpec((1, d), fixed),
            pl.BlockSpec((N_EXPERTS, d), fixed),
            pl.BlockSpec((N_EXPERTS, d), fixed),
            pl.BlockSpec((N_EXPERTS, 1), fixed),
        ],
        out_specs=[
            pl.BlockSpec((tm, d + LANES), row),
            pl.BlockSpec((8, tm), lambda i: (0, i)),
        ],
        out_shape=[
            jax.ShapeDtypeStruct((t, d + LANES), F32),
            jax.ShapeDtypeStruct((8, t), jnp.int32),
        ],
        compiler_params=_cparams(("parallel",)),
        name="mix_out",
    )(a_bf16, w_bf16, h2d, ln_g, ln_b, rwt_hi, rwt_lo, rb)


def _expert_kernel(ea_ref, eb_ref, dst_ref, nused_ref, h_hbm, wga_f32, wua_f32, wda_f32, wgb_f32, wub_f32,
                   wdb_f32, g_ref, b_ref, o_hbm, xbuf, ybuf, wga_ref, wua_ref, wda_ref, wgb_ref, wub_ref,
                   wdb_ref, gsem, ssem, *, n_tokens):
    i = pl.program_id(0)
    tile = xbuf.shape[1]
    d = ybuf.shape[2]
    n_used = nused_ref[0]

    before = jnp.maximum(i - 1, 0)
    new_class = (i == 0) | (ea_ref[i] != ea_ref[before]) | (eb_ref[i] != eb_ref[before])

    @pl.when((i < n_used) & new_class)
    def _():
        for src, dst in ((wga_f32, wga_ref), (wua_f32, wua_ref), (wda_f32, wda_ref),
                         (wgb_f32, wgb_ref), (wub_f32, wub_ref), (wdb_f32, wdb_ref)):
            dst[...] = src[0, 0].astype(BF16)

    def start_gather(step, slot):
        base = step * tile
        for r in range(tile):
            src = jnp.minimum(dst_ref[base + r], n_tokens - 1)
            pltpu.make_async_copy(h_hbm.at[pl.ds(src, 1)], xbuf.at[slot, pl.ds(r, 1)], gsem.at[slot]).start()

    def start_scatter(step, slot):
        base = step * tile
        for r in range(tile):
            dst = dst_ref[base + r]
            pltpu.make_async_copy(ybuf.at[slot, pl.ds(r, 1)], o_hbm.at[pl.ds(dst, 1)], ssem.at[slot]).start()

    def wait_gather(slot):
        pltpu.make_async_copy(h_hbm.at[pl.ds(0, tile)], xbuf.at[slot], gsem.at[slot]).wait()

    def wait_scatter(slot):
        pltpu.make_async_copy(ybuf.at[slot], o_hbm.at[pl.ds(0, tile)], ssem.at[slot]).wait()

    @pl.when(i == 0)
    def _():
        ybuf[...] = jnp.zeros_like(ybuf)
        for slot in range(2):
            fill = pltpu.make_async_copy(ybuf.at[slot], o_hbm.at[pl.ds(n_tokens + slot * tile, tile)], ssem.at[slot])
            fill.start()
            fill.wait()
        start_gather(i, 0)

    def step(slot):
        @pl.when(i + 1 < n_used)
        def _():
            start_gather(i + 1, 1 - slot)

        wait_gather(slot)

        @pl.when(i >= 2)
        def _():
            wait_scatter(slot)

        xe = xbuf[slot]
        x = xe[:, :d]
        xb = x.astype(BF16)

        def expert(wg_ref, wu_ref, wd_ref):
            gate = _dot(xb, wg_ref[...])
            up = _dot(xb, wu_ref[...])
            return _dot((gate * jax.nn.sigmoid(gate) * up).astype(BF16), wd_ref[...])

        y = xe[:, d:d + 1] * expert(wga_ref, wua_ref, wda_ref) + xe[:, d + 1:d + 2] * expert(wgb_ref, wub_ref, wdb_ref)
        ybuf[slot] = _layer_norm_rows(ALPHA * x + y, g_ref[...], b_ref[...])
        start_scatter(i, slot)

        @pl.when(i == n_used - 1)
        def _():
            @pl.when(i >= 1)
            def _():
                wait_scatter(1 - slot)
            wait_scatter(slot)

    @pl.when((i < n_used) & (i % 2 == 0))
    def _():
        step(0)

    @pl.when((i < n_used) & (i % 2 == 1))
    def _():
        step(1)


def _experts(h1x, tile_ea, tile_eb, row_dst, n_used, wg, wu, wd, layer, ln_g, ln_b, n_tokens):
    d = wg.shape[2]
    de = wg.shape[3]
    n_tiles = tile_ea.shape[0]
    ea = lambda i, a, b, rd, nu: (layer, a[i], 0, 0)
    eb = lambda i, a, b, rd, nu: (layer, b[i], 0, 0)
    fixed = lambda i, a, b, rd, nu: (0, 0)
    w_in, w_out = (1, 1, d, de), (1, 1, de, d)
    grid_spec = pltpu.PrefetchScalarGridSpec(
        num_scalar_prefetch=4,
        grid=(n_tiles,),
        in_specs=[
            pl.BlockSpec(memory_space=pl.ANY),
            pl.BlockSpec(w_in, ea), pl.BlockSpec(w_in, ea), pl.BlockSpec(w_out, ea),
            pl.BlockSpec(w_in, eb), pl.BlockSpec(w_in, eb), pl.BlockSpec(w_out, eb),
            pl.BlockSpec((1, d), fixed), pl.BlockSpec((1, d), fixed),
        ],
        out_specs=pl.BlockSpec(memory_space=pl.ANY),
        scratch_shapes=[pltpu.VMEM((2, MOE_TILE, h1x.shape[1]), F32), pltpu.VMEM((2, MOE_TILE, d), F32),
                        pltpu.VMEM((d, de), BF16), pltpu.VMEM((d, de), BF16), pltpu.VMEM((de, d), BF16),
                        pltpu.VMEM((d, de), BF16), pltpu.VMEM((d, de), BF16), pltpu.VMEM((de, d), BF16),
                        pltpu.SemaphoreType.DMA((2,)), pltpu.SemaphoreType.DMA((2,))],
    )
    return pl.pallas_call(
        functools.partial(_expert_kernel, n_tokens=n_tokens),
        grid_spec=grid_spec,
        out_shape=jax.ShapeDtypeStruct((n_tokens + 2 * MOE_TILE, d), F32),
        compiler_params=_cparams(("arbitrary",)),
        name="moe_experts",
    )(tile_ea, tile_eb, row_dst, n_used, h1x, wg, wu, wd, wg, wu, wd, ln_g, ln_b)


def _moe_plan(cls):
    t = cls.shape[0]
    onehot = (cls[:, None] == jnp.arange(N_CLASSES)[None, :]).astype(jnp.int32)
    csum = jnp.cumsum(onehot, axis=0)
    rank = jnp.take_along_axis(csum, cls[:, None], axis=1)[:, 0] - 1
    counts = csum[-1]
    padded = (counts + MOE_TILE - 1) // MOE_TILE * MOE_TILE
    ends = jnp.cumsum(padded)
    pos = (ends - padded)[cls] + rank
    n_tiles = t // MOE_TILE + N_CLASSES
    row = jnp.arange(n_tiles * MOE_TILE, dtype=jnp.int32)
    padding_dst = t + (row // MOE_TILE % 2) * MOE_TILE + row % MOE_TILE
    row_dst = padding_dst.at[pos].set(jnp.arange(t, dtype=jnp.int32))
    tile_start = jnp.arange(n_tiles, dtype=jnp.int32) * MOE_TILE
    tile_cls = jnp.minimum(jnp.sum((tile_start[:, None] >= ends[None, :]).astype(jnp.int32), axis=1), N_CLASSES - 1)
    pair_lo = np.array([a for a in range(EXPERTS_PER_GROUP) for b in range(a + 1, EXPERTS_PER_GROUP)], np.int32)
    pair_hi = np.array([b for a in range(EXPERTS_PER_GROUP) for b in range(a + 1, EXPERTS_PER_GROUP)], np.int32)
    base = tile_cls // N_PAIRS * EXPERTS_PER_GROUP
    tile_ea = (base + jnp.asarray(pair_lo)[tile_cls % N_PAIRS]).astype(jnp.int32)
    tile_eb = (base + jnp.asarray(pair_hi)[tile_cls % N_PAIRS]).astype(jnp.int32)
    n_used = (ends[-1] // MOE_TILE).astype(jnp.int32).reshape(1)
    return tile_ea, tile_eb, row_dst, n_used


def _nsa_proj_kernel(x_ref, w_ref, cos_ref, sin_ref, o_ref, xb_ref, *, n_kv_tiles):
    j = pl.program_id(1)

    @pl.when(j == 0)
    def _():
        xb_ref[...] = x_ref[...].astype(BF16)

    acc = _dot(xb_ref[...], w_ref[...])
    tn = acc.shape[1]
    roped = (j >= n_kv_tiles) | (j % 2 == 0)

    @pl.when(roped)
    def _():
        for hh in range(tn // HEAD_PITCH):
            lanes = slice(hh * HEAD_PITCH, (hh + 1) * HEAD_PITCH)
            a = acc[:, lanes]
            swapped = pltpu.roll(a, HEAD_PITCH // 2, axis=1)
            o_ref[:, lanes] = (a * cos_ref[0, :, lanes] + swapped * sin_ref[0, :, lanes]).astype(o_ref.dtype)

    @pl.when(jnp.logical_not(roped))
    def _():
        lane = lax.broadcasted_iota(jnp.int32, (1, tn), 1)
        o_ref[...] = (acc + jnp.where(lane % HEAD_PITCH == NSA_DH, 1.0, 0.0)).astype(o_ref.dtype)


def _nsa_proj(x2d, w_bf16, seq, t):
    d = x2d.shape[1]
    n = w_bf16.shape[1]
    tm, tn = min(PROJ_ROW_TILE, seq), NSA_GROUPS * HEAD_PITCH
    n_kv_tiles = 6
    k_tabs = _rope_tables_split(seq, NSA_DH, HEAD_PITCH, tn, 1.0)
    q_tabs = _rope_tables_split(seq, NSA_DH, HEAD_PITCH, tn, NSA_DH ** -0.5 * math.log2(math.e))
    cos_t = jnp.stack([k_tabs[0], q_tabs[0]])
    sin_t = jnp.stack([k_tabs[1], q_tabs[1]])
    tiles_per_seq = seq // tm
    tab = lambda i, j: (j // n_kv_tiles, i % tiles_per_seq, 0)
    return pl.pallas_call(
        functools.partial(_nsa_proj_kernel, n_kv_tiles=n_kv_tiles),
        grid=(t // tm, n // tn),
        in_specs=[
            pl.BlockSpec((tm, d), lambda i, j: (i, 0)),
            pl.BlockSpec((d, tn), lambda i, j: (0, j)),
            pl.BlockSpec((1, tm, tn), tab),
            pl.BlockSpec((1, tm, tn), tab),
        ],
        out_specs=pl.BlockSpec((tm, tn), lambda i, j: (i, j)),
        out_shape=jax.ShapeDtypeStruct((t, n), BF16),
        scratch_shapes=[pltpu.VMEM((tm, d), BF16)],
        compiler_params=_cparams(("parallel", "arbitrary")),
        name="nsa_proj",
    )(x2d, w_bf16, cos_t, sin_t)


def _gate_kernel(x_ref, w_ref, o_ref):
    o_ref[...] = jax.nn.sigmoid(_dot(x_ref[...].astype(BF16), w_ref[...]))


def _nsa_gates(x2d, wg_bf16, t):
    d = x2d.shape[1]
    tm = ROW_TILE
    return pl.pallas_call(
        _gate_kernel,
        grid=(t // tm,),
        in_specs=[pl.BlockSpec((tm, d), lambda i: (i, 0)), pl.BlockSpec((d, LANES), lambda i: (0, 0))],
        out_specs=pl.BlockSpec((tm, LANES), lambda i: (i, 0)),
        out_shape=jax.ShapeDtypeStruct((t, LANES), F32),
        compiler_params=_cparams(("parallel",)),
        name="nsa_gates",
    )(x2d, wg_bf16)


def _gelu_tanh(x):
    return 0.5 * x * (1.0 + jnp.tanh(math.sqrt(2.0 / math.pi) * (x + 0.044715 * (x * x * x))))


def _cmp_kernel(x_ref, pe_ref, w1_ref, w2_ref, o_ref):
    x = x_ref[0, 0, 0].astype(F32)
    n = x.shape[0]
    xa = (x + pe_ref[0, 0:1, :]).astype(BF16)
    xb = (x + pe_ref[0, 1:2, :]).astype(BF16)
    u = _dot(xa, w1_ref[0, 0])
    v = _dot(xb, w1_ref[0, 1])
    pre = u + pltpu.roll(v, n - 1, axis=0)
    out = _dot(_gelu_tanh(pre).astype(BF16), w2_ref[0])
    lane = lax.broadcasted_iota(jnp.int32, (1, out.shape[1]), 1)
    ones_lane = jnp.where((lane == NSA_DH) & (pl.program_id(0) == 1), 1.0, 0.0)
    o_ref[0, 0, 0] = (out + ones_lane).astype(o_ref.dtype)


def _compress(xkv, pe, w1, w2):
    _, b, g, n, w = xkv.shape
    pitch = w2.shape[2]
    return pl.pallas_call(
        _cmp_kernel,
        grid=(2, b, g),
        in_specs=[
            pl.BlockSpec((1, 1, 1, n, w), lambda s, bb, gg: (s, bb, gg, 0, 0)),
            pl.BlockSpec((1, 2, w), lambda s, bb, gg: (s, 0, 0)),
            pl.BlockSpec((1, 2, w, CMP_HIDDEN), lambda s, bb, gg: (s, 0, 0, 0)),
            pl.BlockSpec((1, CMP_HIDDEN, pitch), lambda s, bb, gg: (s, 0, 0)),
        ],
        out_specs=pl.BlockSpec((1, 1, 1, n, pitch), lambda s, bb, gg: (s, bb, gg, 0, 0)),
        out_shape=jax.ShapeDtypeStruct((2, b, g, n, pitch), BF16),
        compiler_params=_cparams(("parallel", "parallel", "parallel")),
        name="nsa_compress",
    )(xkv, pe, w1, w2)


def _softmax_pv(s_t, v_t):
    m = jnp.max(s_t, axis=0, keepdims=True)
    p = jnp.exp2(s_t - m).astype(BF16)
    return p, _dot(v_t, p)


def _normalize(ov_t):
    return ov_t * (1.0 / ov_t[NSA_DH:NSA_DH + 1, :])


def _nsa_attn_kernel(q_ref, gate_ref, kc_ref, vc_ref, ks_ref, vs_ref, kw_ref, vw_ref, ovl_ref, xt_ref,
                     o_ref, m_ref, acc_ref, s0_ref, s1_ref, *, seq, top_k):
    i = pl.program_id(2)
    s_refs = (s0_ref, s1_ref)
    qb, rep, pitch = Q_BLOCK, NSA_REP, HEAD_PITCH
    t0 = i * qb
    q2 = q_ref[...]
    q4 = jnp.concatenate([q2[:, r * pitch:(r + 1) * pitch] for r in range(rep)], axis=0)
    t_lane = t0 + lax.broadcasted_iota(jnp.int32, (1, qb), 1)
    t_lane4 = jnp.concatenate([t_lane] * rep, axis=1)

    def add_cols(s_t, bias_t):
        return jnp.concatenate([s_t[:, r * qb:(r + 1) * qb] + bias_t for r in range(rep)], axis=1)

    n_cmp_pad = kc_ref.shape[2]
    cmp_end = lax.broadcasted_iota(jnp.int32, (n_cmp_pad, qb), 0) * CMP_STRIDE + (CMP_LEN - 1)
    s_c = add_cols(_nt_dot(kc_ref[0, 0], q4), jnp.where(cmp_end <= t_lane, 0.0, NEG))
    p_c, ov_c = _softmax_pv(s_c, vc_ref[0, 0, :V_ROWS])
    o_cmp = jnp.where(t_lane4 >= CMP_LEN - 1, _normalize(ov_c), 0.0)

    ovl = ovl_ref[...]
    nsel = ovl.shape[0] - 8
    imp = jnp.zeros((nsel, qb), F32)
    for r in range(rep):
        ext = _dot(ovl, p_c[:, r * qb:(r + 1) * qb])
        imp = imp + ext[:nsel] * (1.0 / ext[nsel:nsel + 1])

    def values_t(v_ref, first_tile, n_tiles):
        return jnp.concatenate([v_ref[0, 0, first_tile + u, :V_ROWS] for u in range(n_tiles)], axis=1)

    wk = min(seq, qb + WINDOW)
    w_start = pl.multiple_of(jnp.maximum(t0 + qb - wk, 0), qb)
    dist = t_lane - (w_start + lax.broadcasted_iota(jnp.int32, (wk, qb), 0))
    bias_w = jnp.where((dist >= 0) & (dist < WINDOW), 0.0, NEG)
    s_w = add_cols(_nt_dot(kw_ref[pl.ds(w_start, wk), :], q4), bias_w)
    _, ov_w = _softmax_pv(s_w, values_t(vw_ref, w_start // LANES, wk // LANES))
    o_win = _normalize(ov_w)

    blk = lax.broadcasted_iota(jnp.int32, (nsel, qb), 0)
    t_col = t0 + lax.broadcasted_iota(jnp.int32, (nsel, qb), 1)
    cur = t_col // SLC_BLOCK
    valid = (blk * SLC_BLOCK <= t_col) & (blk < seq // SLC_BLOCK)
    forced = (blk == 0) | (blk == cur) | (blk == cur - 1)
    score = jnp.where(valid, imp + jnp.where(forced, FORCE_BONUS, 0.0), NEG)
    blk_f = blk.astype(F32)
    chosen = jnp.zeros((nsel, qb), F32)
    for _ in range(top_k):
        m = jnp.max(score, axis=0, keepdims=True)
        first = jnp.min(jnp.where(score == m, blk_f, float(nsel)), axis=0, keepdims=True)
        hit = blk_f == first
        chosen = jnp.where(hit, 1.0, chosen)
        score = jnp.where(hit, -jnp.inf, score)
    chosen = jnp.where(valid, chosen, 0.0)
    sel_bias = ((chosen.T - 1.0) * -NEG).astype(BF16)

    q_aug = jnp.concatenate([q4, jnp.concatenate([sel_bias] * rep, axis=0)], axis=1)
    kc_len = min(SLC_CHUNK, seq)
    last = t0 // kc_len
    k_row = lax.broadcasted_iota(jnp.int32, (kc_len, qb), 0)
    q_lane = lax.broadcasted_iota(jnp.int32, (kc_len, qb), 1) + (t0 - last * kc_len)
    causal = jnp.where(k_row <= q_lane, 0.0, NEG)

    def slc_scores(c, slot):
        start = pl.multiple_of(c * kc_len, kc_len)
        k_aug = jnp.concatenate([ks_ref[pl.ds(start, kc_len), :], xt_ref[pl.ds(start, kc_len), :]], axis=1)
        s = _nt_dot(k_aug, q_aug)
        s_refs[slot][...] = add_cols(s, causal * (c == last).astype(F32))

    def slc_update(c, slot, m_old, acc_old):
        s = s_refs[slot][...]
        m_new = jnp.maximum(m_old, jnp.max(s, axis=0, keepdims=True))
        p = jnp.exp2(s - m_new).astype(BF16)
        v_t = values_t(vs_ref, c * (kc_len // LANES), kc_len // LANES)
        return m_new, jnp.exp2(m_old - m_new) * acc_old + _dot(v_t, p)

    def chunk_pair(j, carry):
        slc_scores(2 * j + 1, 1)
        m, acc = slc_update(2 * j, 0, m_ref[...], acc_ref[...])
        slc_scores(jnp.minimum(2 * j + 2, last), 0)
        m, acc = slc_update(2 * j + 1, 1, m, acc)
        m_ref[...] = m
        acc_ref[...] = acc
        return carry

    m_ref[...] = jnp.full(m_ref.shape, NEG, F32)
    acc_ref[...] = jnp.zeros_like(acc_ref)
    slc_scores(0, 0)
    lax.fori_loop(0, (last + 1) // 2, chunk_pair, 0)

    @pl.when(last % 2 == 0)
    def _():
        m, acc = slc_update(last, 0, m_ref[...], acc_ref[...])
        m_ref[...] = m
        acc_ref[...] = acc

    o_slc = _normalize(acc_ref[...])

    gates = gate_ref[0, 0]
    head_lane = lax.broadcasted_iota(jnp.int32, (V_ROWS, qb), 0) < NSA_DH
    no_lanes = jnp.zeros((pitch - V_ROWS, qb), F32)
    outs = []
    for r in range(rep):
        cols = slice(r * qb, (r + 1) * qb)
        o_r = (gates[3 * r:3 * r + 1] * o_cmp[:, cols] + gates[3 * r + 1:3 * r + 2] * o_slc[:, cols]
               + gates[3 * r + 2:3 * r + 3] * o_win[:, cols])
        outs.append(jnp.concatenate([jnp.where(head_lane, o_r, 0.0), no_lanes], axis=0).T)
    o_ref[...] = jnp.concatenate(outs, axis=1).astype(o_ref.dtype)


def _nsa_attention(kvq, gates_t, k_cmp, v_cmp_t, vs_t, vw_t, batch, seq):
    t = kvq.shape[0]
    g, rep, pitch = NSA_GROUPS, NSA_REP, HEAD_PITCH
    nq = seq // Q_BLOCK
    n_cmp_pad = k_cmp.shape[2]
    nsel = seq // SLC_BLOCK
    nsel_pad = max(LANES, nsel)
    ci = np.arange(n_cmp_pad)[None, :]
    sj = np.arange(nsel_pad)[:, None]
    overlap_t = ((ci * CMP_STRIDE < (sj + 1) * SLC_BLOCK) & (ci * CMP_STRIDE + CMP_LEN > sj * SLC_BLOCK)
                 & (ci < n_cmp_pad - 1) & (sj < nsel))
    ovl = jnp.asarray(np.concatenate([overlap_t, np.ones((8, n_cmp_pad), bool)]), BF16)
    kc_len = min(SLC_CHUNK, seq)
    key_block = jnp.asarray((np.arange(seq) // SLC_BLOCK)[:, None] == np.arange(nsel_pad)[None, :], BF16)
    kern = functools.partial(_nsa_attn_kernel, seq=seq, top_k=min(SLC_TOPK, nsel))
    per_bg = lambda b, gg, i: (b, gg, 0, 0)
    fixed = lambda b, gg, i: (0, 0)

    def keys(idx):
        return pl.BlockSpec((seq, pitch), lambda b, gg, i: (b, idx * g + gg))

    values = pl.BlockSpec((1, 1, seq // LANES, pitch, LANES), lambda b, gg, i: (b, gg, 0, 0, 0))
    return pl.pallas_call(
        kern,
        grid=(batch, g, nq),
        in_specs=[
            pl.BlockSpec((Q_BLOCK, rep * pitch), lambda b, gg, i: (b * nq + i, 6 + gg)),
            pl.BlockSpec((1, 1, 3 * rep, Q_BLOCK), lambda b, gg, i: (b, gg, 0, i)),
            pl.BlockSpec((1, 1, n_cmp_pad, pitch), per_bg),
            pl.BlockSpec((1, 1, pitch, n_cmp_pad), per_bg),
            keys(2), values, keys(4), values,
            pl.BlockSpec(ovl.shape, fixed),
            pl.BlockSpec(key_block.shape, fixed),
        ],
        out_specs=pl.BlockSpec((Q_BLOCK, rep * pitch), lambda b, gg, i: (b * nq + i, gg)),
        out_shape=jax.ShapeDtypeStruct((t, NSA_HEADS * pitch), BF16),
        scratch_shapes=[pltpu.VMEM((1, rep * Q_BLOCK), F32), pltpu.VMEM((V_ROWS, rep * Q_BLOCK), F32),
                        pltpu.VMEM((kc_len, rep * Q_BLOCK), F32), pltpu.VMEM((kc_len, rep * Q_BLOCK), F32)],
        compiler_params=_cparams(("parallel", "parallel", "arbitrary")),
        name="nsa_attention",
    )(kvq, gates_t, k_cmp, v_cmp_t, kvq, vs_t, kvq, vw_t, ovl, key_block)


def _moe_layer(h1x, cls, wg, wu, wd, layer, ln_g, ln_b):
    tile_ea, tile_eb, row_dst, n_used = _moe_plan(cls[0])
    return _experts(h1x, tile_ea, tile_eb, row_dst, n_used, wg, wu, wd, layer, ln_g, ln_b, h1x.shape[0])


def kernel(x, ret_w_in, ret_w_out, nsa_w_kv, cmp_pe_k, cmp_pe_v, cmp_k_w1, cmp_k_w2, cmp_v_w1, cmp_v_w2,
           nsa_w_in, nsa_w_out, router_w, router_b, moe_w_gate, moe_w_up, moe_w_down,
           ln_mix_g, ln_mix_b, ln_ffn_g, ln_ffn_b):
    batch, seq, d = x.shape
    t = batch * seq
    h = x.reshape(t, d)

    rwt = router_w.T
    rwt_hi = rwt.astype(BF16)
    rwt_lo = (rwt - rwt_hi.astype(F32)).astype(BF16)
    rb = router_b.reshape(N_EXPERTS, 1).astype(F32)
    row = lambda v: v.reshape(1, d)

    proj = _ret_proj(h, ret_w_in[0].astype(BF16), seq)
    gated = _ret_core(proj, batch, seq)
    h1x, cls = _mix_out(gated, ret_w_out[0].astype(BF16), h, row(ln_mix_g[0]), row(ln_mix_b[0]),
                        rwt_hi, rwt_lo, rb)
    h = _moe_layer(h1x, cls, moe_w_gate, moe_w_up, moe_w_down, 0,
                   row(ln_ffn_g[0]), row(ln_ffn_b[0]))

    g, dh, rep, pitch = NSA_GROUPS, NSA_DH, NSA_REP, HEAD_PITCH
    n_q = NSA_HEADS * dh

    def pitch_cols(w):
        heads = w.shape[-1] // dh
        lead = w.shape[:-1]
        padded = jnp.pad(w.reshape(lead + (heads, dh)), [(0, 0)] * (len(lead) + 1) + [(0, pitch - dh)])
        return padded.reshape(lead + (heads * pitch,))

    kv_parts = [nsa_w_kv[:, p * KV_W:(p + 1) * KV_W] for p in range(6)]
    w_all = jnp.concatenate(
        [_split_halves(w, dh, pitch) if p % 2 == 0 else pitch_cols(w) for p, w in enumerate(kv_parts)]
        + [_split_halves(nsa_w_in[0][:, :n_q], dh, pitch)], axis=1).astype(BF16)
    kvq = _nsa_proj(h, w_all, seq, t)
    w_gate = jnp.pad(nsa_w_in[0][:, n_q:], ((0, 0), (0, LANES - 3 * NSA_HEADS))).astype(BF16)
    gates = _nsa_gates(h, w_gate, t)[:, :3 * NSA_HEADS]
    gates_t = gates.reshape(batch, seq, g, 3 * rep).transpose(0, 2, 3, 1)

    def blocks16(p, split):
        p = p.reshape(batch, seq // CMP_STRIDE, CMP_STRIDE, g, pitch)
        if split:
            p = jnp.concatenate([p[..., :dh // 2], p[..., pitch // 2:pitch // 2 + dh // 2]], axis=-1)
        else:
            p = p[..., :dh]
        return p.transpose(0, 3, 1, 2, 4).reshape(batch, g, seq // CMP_STRIDE, CMP_STRIDE * dh)

    gw = g * pitch
    xkv = jnp.stack([blocks16(kvq[:, :gw], True), blocks16(kvq[:, gw:2 * gw], False)])
    half_w = CMP_STRIDE * dh
    pe = jnp.stack([cmp_pe_k.reshape(2, half_w), cmp_pe_v.reshape(2, half_w)]).astype(F32)
    w1 = jnp.stack([cmp_k_w1.reshape(2, half_w, CMP_HIDDEN), cmp_v_w1.reshape(2, half_w, CMP_HIDDEN)]).astype(BF16)
    w2 = jnp.stack([_split_halves(cmp_k_w2, dh, pitch), pitch_cols(cmp_v_w2)]).astype(BF16)
    kv_cmp = _compress(xkv, pe, w1, w2)

    def keys_on_lanes(p):
        return p.reshape(batch, seq // LANES, LANES, g, pitch).transpose(0, 3, 1, 4, 2)

    attn = _nsa_attention(kvq, gates_t, kv_cmp[0], kv_cmp[1].transpose(0, 1, 3, 2),
                          keys_on_lanes(kvq[:, 3 * gw:4 * gw]), keys_on_lanes(kvq[:, 5 * gw:6 * gw]),
                          batch, seq)
    w_out = jnp.pad(nsa_w_out[0].reshape(NSA_HEADS, dh, d), ((0, 0), (0, pitch - dh), (0, 0))).reshape(
        NSA_HEADS * pitch, d).astype(BF16)
    h1x, cls = _mix_out(attn, w_out, h, row(ln_mix_g[1]), row(ln_mix_b[1]), rwt_hi, rwt_lo, rb)
    h = _moe_layer(h1x, cls, moe_w_gate, moe_w_up, moe_w_down, 1, row(ln_ffn_g[1]), row(ln_ffn_b[1]))
    return h[:t].reshape(batch, seq, d)
```

```python
import functools
import math

import jax
import jax.numpy as jnp
import numpy as np
from jax import lax
from jax.experimental import pallas as pl
from jax.experimental.pallas import tpu as pltpu

F32 = jnp.float32
BF16 = jnp.bfloat16

D_MODEL = 1024
DEPTH = 2
ALPHA = (2.0 * DEPTH) ** 0.25
LN_EPS = 1e-5
ROPE_THETA = 10000.0
NEG = -1e30

RET_HEADS = 4
RET_DK = D_MODEL // RET_HEADS
RET_DV = 2 * RET_DK
RET_CHUNK = 256
RET_HEADS_PER_STEP = 4

NSA_HEADS = 16
NSA_GROUPS = 4
NSA_REP = NSA_HEADS // NSA_GROUPS
NSA_DH = D_MODEL // NSA_HEADS
KV_W = NSA_GROUPS * NSA_DH
CMP_LEN = 32
CMP_STRIDE = 16
CMP_HIDDEN = 256
SLC_BLOCK = 64
SLC_TOPK = 16
WINDOW = 512
Q_BLOCK = 256
FORCE_BONUS = 1e3
SLC_CHUNK = 512
SLC_SUBCHUNK = 256

N_EXPERTS = 16
N_EXPERT_GROUPS = 4
EXPERTS_PER_GROUP = N_EXPERTS // N_EXPERT_GROUPS
N_PAIRS = EXPERTS_PER_GROUP * (EXPERTS_PER_GROUP - 1) // 2
N_CLASSES = N_EXPERT_GROUPS * N_PAIRS
D_EXPERT = 512

LANES = 128
HEAD_PITCH = LANES
V_ROWS = NSA_DH + 16
ROW_TILE = 512
PROJ_ROW_TILE = 1024
MOE_TILE = 256
VMEM_LIMIT = 56 * 1024 * 1024


def _cparams(sem):
    return pltpu.CompilerParams(dimension_semantics=sem, vmem_limit_bytes=VMEM_LIMIT)


def _nt_dot(a, b):
    return lax.dot_general(a, b, (((1,), (1,)), ((), ())), preferred_element_type=F32)


def _dot(a, b):
    return jnp.dot(a, b, preferred_element_type=F32)


def _rope_tables_half(seq, d):
    inv = ROPE_THETA ** (-jnp.arange(0, d, 2, dtype=F32) / d)
    ang = jnp.arange(seq, dtype=F32)[:, None] * inv[None, :]
    return jnp.cos(ang), jnp.sin(ang)


def _split_halves(a, d, pitch):
    lead = a.shape[:-1]
    heads = a.shape[-1] // d
    a = a.reshape(lead + (heads, 2, d // 2))
    a = jnp.pad(a, [(0, 0)] * (len(lead) + 2) + [(0, (pitch - d) // 2)])
    return a.reshape(lead + (heads * pitch,))


def _rope_tables_split(seq, d, pitch, width, scale):
    cos, sin = _rope_tables_half(seq, d)
    reps = width // pitch
    cos_t = jnp.tile(_split_halves(jnp.concatenate([cos, cos], -1), d, pitch), (1, reps))
    sin_t = jnp.tile(_split_halves(jnp.concatenate([-sin, sin], -1), d, pitch), (1, reps))
    return cos_t * scale, sin_t * scale


def _ret_proj_kernel(x_ref, w_ref, cos_ref, sin_ref, o_ref, xb_ref, *, n_rope_tiles, k_scale):
    j = pl.program_id(1)

    @pl.when(j == 0)
    def _():
        xb_ref[...] = x_ref[...].astype(BF16)

    acc = _dot(xb_ref[...], w_ref[...])
    tn = acc.shape[1]
    half = RET_DK // 2

    @pl.when(j < n_rope_tiles)
    def _():
        c = cos_ref[...]
        s = sin_ref[...]
        scale = jnp.where(j >= n_rope_tiles // 2, k_scale, 1.0).astype(F32)
        for hh in range(tn // RET_DK):
            lo = hh * RET_DK
            x1 = acc[:, lo:lo + half]
            x2 = acc[:, lo + half:lo + RET_DK]
            o_ref[:, lo:lo + half] = ((x1 * c - x2 * s) * scale).astype(o_ref.dtype)
            o_ref[:, lo + half:lo + RET_DK] = ((x2 * c + x1 * s) * scale).astype(o_ref.dtype)

    @pl.when(j >= n_rope_tiles)
    def _():
        o_ref[...] = acc.astype(o_ref.dtype)


def _ret_proj(x2d, w_bf16, seq):
    t, d = x2d.shape
    n = w_bf16.shape[1]
    tm, tn = min(PROJ_ROW_TILE, seq), 1024
    cos, sin = _rope_tables_half(seq, RET_DK)
    tiles_per_seq = seq // tm
    n_rope_tiles = 2 * RET_HEADS * RET_DK // tn
    kern = functools.partial(_ret_proj_kernel, n_rope_tiles=n_rope_tiles, k_scale=RET_DK ** -0.5)
    return pl.pallas_call(
        kern,
        grid=(t // tm, n // tn),
        in_specs=[
            pl.BlockSpec((tm, d), lambda i, j: (i, 0)),
            pl.BlockSpec((d, tn), lambda i, j: (0, j)),
            pl.BlockSpec((tm, RET_DK // 2), lambda i, j: (i % tiles_per_seq, 0)),
            pl.BlockSpec((tm, RET_DK // 2), lambda i, j: (i % tiles_per_seq, 0)),
        ],
        out_specs=pl.BlockSpec((tm, tn), lambda i, j: (i, j)),
        out_shape=jax.ShapeDtypeStruct((t, n), BF16),
        scratch_shapes=[pltpu.VMEM((tm, d), BF16)],
        compiler_params=_cparams(("parallel", "arbitrary")),
        name="ret_proj",
    )(x2d, w_bf16, cos, sin)


def _ret_core_kernel(q_ref, k_ref, v_ref, g_ref, mask_ref, qd_ref, kd_ref, cd_ref, o_ref, state_ref):
    c = pl.program_id(2)
    dk, dv = state_ref.shape[1], state_ref.shape[2]

    @pl.when(c == 0)
    def _():
        state_ref[...] = jnp.zeros_like(state_ref)

    for hh in range(state_ref.shape[0]):
        q = q_ref[:, hh * dk:(hh + 1) * dk]
        k = k_ref[:, hh * dk:(hh + 1) * dk]
        v = v_ref[:, hh * dv:(hh + 1) * dv]
        inner = _nt_dot(q, k) * mask_ref[hh]
        state = state_ref[hh]
        qs = (q.astype(F32) * qd_ref[hh]).astype(BF16)
        o = _dot(inner.astype(BF16), v) + _dot(qs, state.astype(BF16))
        ks = k.astype(F32) * kd_ref[hh]
        state_ref[hh] = state * cd_ref[hh] + _dot(ks.T.astype(BF16), v)

        mu = jnp.mean(o, axis=-1, keepdims=True)
        var = jnp.mean(jnp.square(o - mu), axis=-1, keepdims=True)
        on = (o - mu) * lax.rsqrt(var + LN_EPS)
        g = g_ref[:, hh * dv:(hh + 1) * dv].astype(F32)
        o_ref[:, hh * dv:(hh + 1) * dv] = (g * jax.nn.sigmoid(g) * on).astype(o_ref.dtype)


def _ret_core(proj, batch, seq):
    t = proj.shape[0]
    h, dk, dv, c = RET_HEADS, RET_DK, RET_DV, min(RET_CHUNK, seq)
    nc = seq // c
    log_gamma = jnp.log1p(-(2.0 ** (-5.0 - jnp.arange(h, dtype=F32))))
    idx = jnp.arange(c, dtype=F32)
    diff = idx[:, None] - idx[None, :]
    mask = jnp.where(diff >= 0, jnp.exp(log_gamma[:, None, None] * jnp.maximum(diff, 0.0)), 0.0)
    qd = jnp.broadcast_to(jnp.exp(log_gamma[:, None] * (idx[None, :] + 1.0))[:, :, None], (h, c, dk))
    kd = jnp.broadcast_to(jnp.exp(log_gamma[:, None] * (c - 1.0 - idx[None, :]))[:, :, None], (h, c, dk))
    cd = jnp.broadcast_to(jnp.exp(log_gamma * c)[:, None, None], (h, 1, dv))
    hs = RET_HEADS_PER_STEP
    hg = h // hs
    k_off = h * dk // (hs * dk)
    v_off = 2 * h * dk // (hs * dv)
    g_off = (2 * h * dk + h * dv) // (hs * dv)
    return pl.pallas_call(
        _ret_core_kernel,
        grid=(batch, hg, nc),
        in_specs=[
            pl.BlockSpec((c, hs * dk), lambda b, hh, cc: (b * nc + cc, hh)),
            pl.BlockSpec((c, hs * dk), lambda b, hh, cc: (b * nc + cc, k_off + hh)),
            pl.BlockSpec((c, hs * dv), lambda b, hh, cc: (b * nc + cc, v_off + hh)),
            pl.BlockSpec((c, hs * dv), lambda b, hh, cc: (b * nc + cc, g_off + hh)),
            pl.BlockSpec((hs, c, c), lambda b, hh, cc: (hh, 0, 0)),
            pl.BlockSpec((hs, c, dk), lambda b, hh, cc: (hh, 0, 0)),
            pl.BlockSpec((hs, c, dk), lambda b, hh, cc: (hh, 0, 0)),
            pl.BlockSpec((hs, 1, dv), lambda b, hh, cc: (hh, 0, 0)),
        ],
        out_specs=pl.BlockSpec((c, hs * dv), lambda b, hh, cc: (b * nc + cc, hh)),
        out_shape=jax.ShapeDtypeStruct((t, h * dv), BF16),
        scratch_shapes=[pltpu.VMEM((hs, dk, dv), F32)],
        compiler_params=_cparams(("parallel", "parallel", "arbitrary")),
        name="ret_core",
    )(proj, proj, proj, proj, mask, qd, kd, cd)


def _layer_norm_rows(y, g, b):
    mu = jnp.mean(y, axis=-1, keepdims=True)
    var = jnp.mean(jnp.square(y - mu), axis=-1, keepdims=True)
    return (y - mu) * lax.rsqrt(var + LN_EPS) * g + b


def _split_bf16(x):
    hi = x.astype(BF16)
    lo = (x - hi.astype(F32)).astype(BF16)
    return hi, lo


def _route(hn, rwt_hi_ref, rwt_lo_ref, rb_ref, cls_ref):
    h_hi, h_lo = _split_bf16(hn)
    w_hi = rwt_hi_ref[...]
    logits = _nt_dot(w_hi, h_hi) + _nt_dot(w_hi, h_lo) + _nt_dot(rwt_lo_ref[...], h_hi)
    aff = jax.nn.sigmoid(logits)
    sel = aff + rb_ref[...]
    tm = hn.shape[0]

    def row(a, e):
        return a[e:e + 1, :]

    best_g = jnp.zeros((1, tm), jnp.int32)
    best_s = None
    for gi in range(N_EXPERT_GROUPS):
        a0, a1, a2, a3 = [row(sel, EXPERTS_PER_GROUP * gi + e) for e in range(EXPERTS_PER_GROUP)]
        hi01, lo01 = jnp.maximum(a0, a1), jnp.minimum(a0, a1)
        hi23, lo23 = jnp.maximum(a2, a3), jnp.minimum(a2, a3)
        top2 = jnp.maximum(hi01, hi23) + jnp.maximum(jnp.minimum(hi01, hi23), jnp.maximum(lo01, lo23))
        if best_s is None:
            best_s = top2
        else:
            upd = top2 > best_s
            best_g = jnp.where(upd, gi, best_g)
            best_s = jnp.where(upd, top2, best_s)

    def pick(a, e):
        out = row(a, e)
        for gi in range(1, N_EXPERT_GROUPS):
            out = jnp.where(best_g == gi, row(a, EXPERTS_PER_GROUP * gi + e), out)
        return out

    cs = [pick(sel, e) for e in range(EXPERTS_PER_GROUP)]
    af = [pick(aff, e) for e in range(EXPERTS_PER_GROUP)]
    i1 = jnp.zeros((1, tm), jnp.int32)
    s1, a1v = cs[0], af[0]
    for e in range(1, EXPERTS_PER_GROUP):
        upd = cs[e] > s1
        i1 = jnp.where(upd, e, i1)
        s1 = jnp.where(upd, cs[e], s1)
        a1v = jnp.where(upd, af[e], a1v)
    i2 = jnp.full((1, tm), -1, jnp.int32)
    s2 = jnp.full((1, tm), -jnp.inf, F32)
    a2v = jnp.zeros((1, tm), F32)
    for e in range(EXPERTS_PER_GROUP):
        upd = (i1 != e) & ((i2 < 0) | (cs[e] > s2))
        i2 = jnp.where(upd, e, i2)
        s2 = jnp.where(upd, cs[e], s2)
        a2v = jnp.where(upd, af[e], a2v)
    tot = a1v + a2v
    w1, w2 = a1v / tot, a2v / tot
    first_is_lo = i1 < i2
    lo = jnp.where(first_is_lo, i1, i2)
    hi = jnp.where(first_is_lo, i2, i1)
    pair = jnp.right_shift(lo * (2 * EXPERTS_PER_GROUP - 1 - lo), 1) + (hi - lo - 1)
    cls = best_g * N_PAIRS + pair
    cls_ref[...] = jnp.concatenate([cls, jnp.zeros((7, tm), jnp.int32)], axis=0)
    w_lo = jnp.where(first_is_lo, w1, w2)
    w_hi_ = jnp.where(first_is_lo, w2, w1)
    return jnp.concatenate([w_lo, w_hi_, jnp.zeros((LANES - 2, tm), F32)], axis=0).T


def _mix_out_kernel(a_ref, w_ref, h_ref, g_ref, b_ref, rwt_hi_ref, rwt_lo_ref, rb_ref, o_ref, cls_ref):
    mix = _dot(a_ref[...], w_ref[...])
    hn = _layer_norm_rows(ALPHA * h_ref[...] + mix, g_ref[...], b_ref[...])
    d = hn.shape[1]
    o_ref[:, :d] = hn
    o_ref[:, d:] = _route(hn, rwt_hi_ref, rwt_lo_ref, rb_ref, cls_ref)


def _mix_out(a_bf16, w_bf16, h2d, ln_g, ln_b, rwt_hi, rwt_lo, rb):
    t = a_bf16.shape[0]
    d = h2d.shape[1]
    k = a_bf16.shape[1]
    tm = ROW_TILE
    row = lambda i: (i, 0)
    fixed = lambda i: (0, 0)
    return pl.pallas_call(
        _mix_out_kernel,
        grid=(t // tm,),
        in_specs=[
            pl.BlockSpec((tm, k), row),
            pl.BlockSpec((k, d), fixed),
            pl.BlockSpec((tm, d), row),
            pl.BlockSpec((1, d), fixed),
            pl.BlockSpec((1, d), fixed),
            pl.BlockSpec((N_EXPERTS, d), fixed),
            pl.BlockSpec((N_EXPERTS, d), fixed),
            pl.BlockSpec((N_EXPERTS, 1), fixed),
        ],
        out_specs=[
            pl.BlockSpec((tm, d + LANES), row),
            pl.BlockSpec((8, tm), lambda i: (0, i)),
        ],
        out_shape=[
            jax.ShapeDtypeStruct((t, d + LANES), F32),
            jax.ShapeDtypeStruct((8, t), jnp.int32),
        ],
        compiler_params=_cparams(("parallel",)),
        name="mix_out",
    )(a_bf16, w_bf16, h2d, ln_g, ln_b, rwt_hi, rwt_lo, rb)


def _expert_kernel(ea_ref, eb_ref, dst_ref, nused_ref, h_hbm, wga_f32, wua_f32, wda_f32, wgb_f32, wub_f32,
                   wdb_f32, g_ref, b_ref, o_hbm, xbuf, ybuf, wga_ref, wua_ref, wda_ref, wgb_ref, wub_ref,
                   wdb_ref, gsem, ssem, *, n_tokens):
    i = pl.program_id(0)
    tile = xbuf.shape[1]
    d = ybuf.shape[2]
    n_used = nused_ref[0]

    before = jnp.maximum(i - 1, 0)
    new_class = (i == 0) | (ea_ref[i] != ea_ref[before]) | (eb_ref[i] != eb_ref[before])

    @pl.when((i < n_used) & new_class)
    def _():
        for src, dst in ((wga_f32, wga_ref), (wua_f32, wua_ref), (wda_f32, wda_ref),
                         (wgb_f32, wgb_ref), (wub_f32, wub_ref), (wdb_f32, wdb_ref)):
            dst[...] = src[0, 0].astype(BF16)

    def start_gather(step, slot):
        base = step * tile
        for r in range(tile):
            src = jnp.minimum(dst_ref[base + r], n_tokens - 1)
            pltpu.make_async_copy(h_hbm.at[pl.ds(src, 1)], xbuf.at[slot, pl.ds(r, 1)], gsem.at[slot]).start()

    def start_scatter(step, slot):
        base = step * tile
        for r in range(tile):
            dst = dst_ref[base + r]
            pltpu.make_async_copy(ybuf.at[slot, pl.ds(r, 1)], o_hbm.at[pl.ds(dst, 1)], ssem.at[slot]).start()

    def wait_gather(slot):
        pltpu.make_async_copy(h_hbm.at[pl.ds(0, tile)], xbuf.at[slot], gsem.at[slot]).wait()

    def wait_scatter(slot):
        pltpu.make_async_copy(ybuf.at[slot], o_hbm.at[pl.ds(0, tile)], ssem.at[slot]).wait()

    @pl.when(i == 0)
    def _():
        ybuf[...] = jnp.zeros_like(ybuf)
        for slot in range(2):
            fill = pltpu.make_async_copy(ybuf.at[slot], o_hbm.at[pl.ds(n_tokens + slot * tile, tile)], ssem.at[slot])
            fill.start()
            fill.wait()
        start_gather(i, 0)

    def step(slot):
        @pl.when(i + 1 < n_used)
        def _():
            start_gather(i + 1, 1 - slot)

        wait_gather(slot)

        @pl.when(i >= 2)
        def _():
            wait_scatter(slot)

        xe = xbuf[slot]
        x = xe[:, :d]
        xb = x.astype(BF16)

        def expert(wg_ref, wu_ref, wd_ref):
            gate = _dot(xb, wg_ref[...])
            up = _dot(xb, wu_ref[...])
            return _dot((gate * jax.nn.sigmoid(gate) * up).astype(BF16), wd_ref[...])

        y = xe[:, d:d + 1] * expert(wga_ref, wua_ref, wda_ref) + xe[:, d + 1:d + 2] * expert(wgb_ref, wub_ref, wdb_ref)
        ybuf[slot] = _layer_norm_rows(ALPHA * x + y, g_ref[...], b_ref[...])
        start_scatter(i, slot)

        @pl.when(i == n_used - 1)
        def _():
            @pl.when(i >= 1)
            def _():
                wait_scatter(1 - slot)
            wait_scatter(slot)

    @pl.when((i < n_used) & (i % 2 == 0))
    def _():
        step(0)

    @pl.when((i < n_used) & (i % 2 == 1))
    def _():
        step(1)


def _experts(h1x, tile_ea, tile_eb, row_dst, n_used, wg, wu, wd, layer, ln_g, ln_b, n_tokens):
    d = wg.shape[2]
    de = wg.shape[3]
    n_tiles = tile_ea.shape[0]
    ea = lambda i, a, b, rd, nu: (layer, a[i], 0, 0)
    eb = lambda i, a, b, rd, nu: (layer, b[i], 0, 0)
    fixed = lambda i, a, b, rd, nu: (0, 0)
    w_in, w_out = (1, 1, d, de), (1, 1, de, d)
    grid_spec = pltpu.PrefetchScalarGridSpec(
        num_scalar_prefetch=4,
        grid=(n_tiles,),
        in_specs=[
            pl.BlockSpec(memory_space=pl.ANY),
            pl.BlockSpec(w_in, ea), pl.BlockSpec(w_in, ea), pl.BlockSpec(w_out, ea),
            pl.BlockSpec(w_in, eb), pl.BlockSpec(w_in, eb), pl.BlockSpec(w_out, eb),
            pl.BlockSpec((1, d), fixed), pl.BlockSpec((1, d), fixed),
        ],
        out_specs=pl.BlockSpec(memory_space=pl.ANY),
        scratch_shapes=[pltpu.VMEM((2, MOE_TILE, h1x.shape[1]), F32), pltpu.VMEM((2, MOE_TILE, d), F32),
                        pltpu.VMEM((d, de), BF16), pltpu.VMEM((d, de), BF16), pltpu.VMEM((de, d), BF16),
                        pltpu.VMEM((d, de), BF16), pltpu.VMEM((d, de), BF16), pltpu.VMEM((de, d), BF16),
                        pltpu.SemaphoreType.DMA((2,)), pltpu.SemaphoreType.DMA((2,))],
    )
    return pl.pallas_call(
        functools.partial(_expert_kernel, n_tokens=n_tokens),
        grid_spec=grid_spec,
        out_shape=jax.ShapeDtypeStruct((n_tokens + 2 * MOE_TILE, d), F32),
        compiler_params=_cparams(("arbitrary",)),
        name="moe_experts",
    )(tile_ea, tile_eb, row_dst, n_used, h1x, wg, wu, wd, wg, wu, wd, ln_g, ln_b)


def _moe_plan(cls):
    t = cls.shape[0]
    onehot = (cls[:, None] == jnp.arange(N_CLASSES)[None, :]).astype(jnp.int32)
    csum = jnp.cumsum(onehot, axis=0)
    rank = jnp.take_along_axis(csum, cls[:, None], axis=1)[:, 0] - 1
    counts = csum[-1]
    padded = (counts + MOE_TILE - 1) // MOE_TILE * MOE_TILE
    ends = jnp.cumsum(padded)
    pos = (ends - padded)[cls] + rank
    n_tiles = t // MOE_TILE + N_CLASSES
    row = jnp.arange(n_tiles * MOE_TILE, dtype=jnp.int32)
    padding_dst = t + (row // MOE_TILE % 2) * MOE_TILE + row % MOE_TILE
    row_dst = padding_dst.at[pos].set(jnp.arange(t, dtype=jnp.int32))
    tile_start = jnp.arange(n_tiles, dtype=jnp.int32) * MOE_TILE
    tile_cls = jnp.minimum(jnp.sum((tile_start[:, None] >= ends[None, :]).astype(jnp.int32), axis=1), N_CLASSES - 1)
    pair_lo = np.array([a for a in range(EXPERTS_PER_GROUP) for b in range(a + 1, EXPERTS_PER_GROUP)], np.int32)
    pair_hi = np.array([b for a in range(EXPERTS_PER_GROUP) for b in range(a + 1, EXPERTS_PER_GROUP)], np.int32)
    base = tile_cls // N_PAIRS * EXPERTS_PER_GROUP
    tile_ea = (base + jnp.asarray(pair_lo)[tile_cls % N_PAIRS]).astype(jnp.int32)
    tile_eb = (base + jnp.asarray(pair_hi)[tile_cls % N_PAIRS]).astype(jnp.int32)
    n_used = (ends[-1] // MOE_TILE).astype(jnp.int32).reshape(1)
    return tile_ea, tile_eb, row_dst, n_used


def _nsa_proj_kernel(x_ref, w_ref, cos_ref, sin_ref, o_ref, xb_ref, *, n_kv_tiles):
    j = pl.program_id(1)

    @pl.when(j == 0)
    def _():
        xb_ref[...] = x_ref[...].astype(BF16)

    acc = _dot(xb_ref[...], w_ref[...])
    tn = acc.shape[1]
    roped = (j >= n_kv_tiles) | (j % 2 == 0)

    @pl.when(roped)
    def _():
        for hh in range(tn // HEAD_PITCH):
            lanes = slice(hh * HEAD_PITCH, (hh + 1) * HEAD_PITCH)
            a = acc[:, lanes]
            swapped = pltpu.roll(a, HEAD_PITCH // 2, axis=1)
            o_ref[:, lanes] = (a * cos_ref[0, :, lanes] + swapped * sin_ref[0, :, lanes]).astype(o_ref.dtype)

    @pl.when(jnp.logical_not(roped))
    def _():
        lane = lax.broadcasted_iota(jnp.int32, (1, tn), 1)
        o_ref[...] = (acc + jnp.where(lane % HEAD_PITCH == NSA_DH, 1.0, 0.0)).astype(o_ref.dtype)


def _nsa_proj(x2d, w_bf16, seq, t):
    d = x2d.shape[1]
    n = w_bf16.shape[1]
    tm, tn = min(PROJ_ROW_TILE, seq), NSA_GROUPS * HEAD_PITCH
    n_kv_tiles = 6
    k_tabs = _rope_tables_split(seq, NSA_DH, HEAD_PITCH, tn, 1.0)
    q_tabs = _rope_tables_split(seq, NSA_DH, HEAD_PITCH, tn, NSA_DH ** -0.5 * math.log2(math.e))
    cos_t = jnp.stack([k_tabs[0], q_tabs[0]])
    sin_t = jnp.stack([k_tabs[1], q_tabs[1]])
    tiles_per_seq = seq // tm
    tab = lambda i, j: (j // n_kv_tiles, i % tiles_per_seq, 0)
    return pl.pallas_call(
        functools.partial(_nsa_proj_kernel, n_kv_tiles=n_kv_tiles),
        grid=(t // tm, n // tn),
        in_specs=[
            pl.BlockSpec((tm, d), lambda i, j: (i, 0)),
            pl.BlockSpec((d, tn), lambda i, j: (0, j)),
            pl.BlockSpec((1, tm, tn), tab),
            pl.BlockSpec((1, tm, tn), tab),
        ],
        out_specs=pl.BlockSpec((tm, tn), lambda i, j: (i, j)),
        out_shape=jax.ShapeDtypeStruct((t, n), BF16),
        scratch_shapes=[pltpu.VMEM((tm, d), BF16)],
        compiler_params=_cparams(("parallel", "arbitrary")),
        name="nsa_proj",
    )(x2d, w_bf16, cos_t, sin_t)


def _gate_kernel(x_ref, w_ref, o_ref):
    o_ref[...] = jax.nn.sigmoid(_dot(x_ref[...].astype(BF16), w_ref[...]))


def _nsa_gates(x2d, wg_bf16, t):
    d = x2d.shape[1]
    tm = ROW_TILE
    return pl.pallas_call(
        _gate_kernel,
        grid=(t // tm,),
        in_specs=[pl.BlockSpec((tm, d), lambda i: (i, 0)), pl.BlockSpec((d, LANES), lambda i: (0, 0))],
        out_specs=pl.BlockSpec((tm, LANES), lambda i: (i, 0)),
        out_shape=jax.ShapeDtypeStruct((t, LANES), F32),
        compiler_params=_cparams(("parallel",)),
        name="nsa_gates",
    )(x2d, wg_bf16)


def _gelu_tanh(x):
    return 0.5 * x * (1.0 + jnp.tanh(math.sqrt(2.0 / math.pi) * (x + 0.044715 * (x * x * x))))


def _cmp_kernel(x_ref, pe_ref, w1_ref, w2_ref, o_ref):
    x = x_ref[0, 0, 0].astype(F32)
    n = x.shape[0]
    xa = (x + pe_ref[0, 0:1, :]).astype(BF16)
    xb = (x + pe_ref[0, 1:2, :]).astype(BF16)
    u = _dot(xa, w1_ref[0, 0])
    v = _dot(xb, w1_ref[0, 1])
    pre = u + pltpu.roll(v, n - 1, axis=0)
    out = _dot(_gelu_tanh(pre).astype(BF16), w2_ref[0])
    lane = lax.broadcasted_iota(jnp.int32, (1, out.shape[1]), 1)
    ones_lane = jnp.where((lane == NSA_DH) & (pl.program_id(0) == 1), 1.0, 0.0)
    o_ref[0, 0, 0] = (out + ones_lane).astype(o_ref.dtype)


def _compress(xkv, pe, w1, w2):
    _, b, g, n, w = xkv.shape
    pitch = w2.shape[2]
    return pl.pallas_call(
        _cmp_kernel,
        grid=(2, b, g),
        in_specs=[
            pl.BlockSpec((1, 1, 1, n, w), lambda s, bb, gg: (s, bb, gg, 0, 0)),
            pl.BlockSpec((1, 2, w), lambda s, bb, gg: (s, 0, 0)),
            pl.BlockSpec((1, 2, w, CMP_HIDDEN), lambda s, bb, gg: (s, 0, 0, 0)),
            pl.BlockSpec((1, CMP_HIDDEN, pitch), lambda s, bb, gg: (s, 0, 0)),
        ],
        out_specs=pl.BlockSpec((1, 1, 1, n, pitch), lambda s, bb, gg: (s, bb, gg, 0, 0)),
        out_shape=jax.ShapeDtypeStruct((2, b, g, n, pitch), BF16),
        compiler_params=_cparams(("parallel", "parallel", "parallel")),
        name="nsa_compress",
    )(xkv, pe, w1, w2)


def _softmax_pv(s_t, v_t):
    m = jnp.max(s_t, axis=0, keepdims=True)
    p = jnp.exp2(s_t - m).astype(BF16)
    return p, _dot(v_t, p)


def _normalize(ov_t):
    return ov_t * (1.0 / ov_t[NSA_DH:NSA_DH + 1, :])


def _nsa_attn_kernel(q_ref, gate_ref, kc_ref, vc_ref, ks_ref, vs_ref, kw_ref, vw_ref, ovl_ref, xt_ref,
                     o_ref, m_ref, acc_ref, s0_ref, s1_ref, *, seq, top_k):
    i = pl.program_id(2)
    s_refs = (s0_ref, s1_ref)
    qb, rep, pitch = Q_BLOCK, NSA_REP, HEAD_PITCH
    t0 = i * qb
    q2 = q_ref[...]
    q4 = jnp.concatenate([q2[:, r * pitch:(r + 1) * pitch] for r in range(rep)], axis=0)
    t_lane = t0 + lax.broadcasted_iota(jnp.int32, (1, qb), 1)
    t_lane4 = jnp.concatenate([t_lane] * rep, axis=1)

    def add_cols(s_t, bias_t):
        return jnp.concatenate([s_t[:, r * qb:(r + 1) * qb] + bias_t for r in range(rep)], axis=1)

    n_cmp_pad = kc_ref.shape[2]
    cmp_end = lax.broadcasted_iota(jnp.int32, (n_cmp_pad, qb), 0) * CMP_STRIDE + (CMP_LEN - 1)
    s_c = add_cols(_nt_dot(kc_ref[0, 0], q4), jnp.where(cmp_end <= t_lane, 0.0, NEG))
    p_c, ov_c = _softmax_pv(s_c, vc_ref[0, 0, :V_ROWS])
    o_cmp = jnp.where(t_lane4 >= CMP_LEN - 1, _normalize(ov_c), 0.0)

    ovl = ovl_ref[...]
    nsel = ovl.shape[0] - 8
    imp = jnp.zeros((nsel, qb), F32)
    for r in range(rep):
        ext = _dot(ovl, p_c[:, r * qb:(r + 1) * qb])
        imp = imp + ext[:nsel] * (1.0 / ext[nsel:nsel + 1])

    def values_t(v_ref, first_tile, n_tiles):
        return jnp.concatenate([v_ref[0, 0, first_tile + u, :V_ROWS] for u in range(n_tiles)], axis=1)

    wk = min(seq, qb + WINDOW)
    w_start = pl.multiple_of(jnp.maximum(t0 + qb - wk, 0), qb)
    dist = t_lane - (w_start + lax.broadcasted_iota(jnp.int32, (wk, qb), 0))
    bias_w = jnp.where((dist >= 0) & (dist < WINDOW), 0.0, NEG)
    s_w = add_cols(_nt_dot(kw_ref[pl.ds(w_start, wk), :], q4), bias_w)
    _, ov_w = _softmax_pv(s_w, values_t(vw_ref, w_start // LANES, wk // LANES))
    o_win = _normalize(ov_w)

    blk = lax.broadcasted_iota(jnp.int32, (nsel, qb), 0)
    t_col = t0 + lax.broadcasted_iota(jnp.int32, (nsel, qb), 1)
    cur = t_col // SLC_BLOCK
    valid = (blk * SLC_BLOCK <= t_col) & (blk < seq // SLC_BLOCK)
    forced = (blk == 0) | (blk == cur) | (blk == cur - 1)
    score = jnp.where(valid, imp + jnp.where(forced, FORCE_BONUS, 0.0), NEG)
    blk_f = blk.astype(F32)
    for _ in range(top_k):
        m = jnp.max(score, axis=0, keepdims=True)
        first = jnp.min(jnp.where(score == m, blk_f, float(nsel)), axis=0, keepdims=True)
        score = jnp.where(blk_f == first, -jnp.inf, score)
    chosen = jnp.where(valid & (score == -jnp.inf), 1.0, 0.0)
    sel_bias = ((chosen.T - 1.0) * -NEG).astype(BF16)

    q_aug = jnp.concatenate([q4, jnp.concatenate([sel_bias] * rep, axis=0)], axis=1)
    kc_len = min(SLC_CHUNK, seq)
    last = t0 // kc_len
    k_row = lax.broadcasted_iota(jnp.int32, (kc_len, qb), 0)
    q_lane = lax.broadcasted_iota(jnp.int32, (kc_len, qb), 1) + (t0 - last * kc_len)
    causal = jnp.where(k_row <= q_lane, 0.0, NEG)

    def slc_scores(c, slot):
        start = pl.multiple_of(c * kc_len, kc_len)
        k_aug = jnp.concatenate([ks_ref[pl.ds(start, kc_len), :], xt_ref[pl.ds(start, kc_len), :]], axis=1)
        s = _nt_dot(k_aug, q_aug)
        s_refs[slot][...] = add_cols(s, causal * (c == last).astype(F32))

    def slc_update(c, slot, m_old, acc_old):
        s = s_refs[slot][...]
        m_new = jnp.maximum(m_old, jnp.max(s, axis=0, keepdims=True))
        p = jnp.exp2(s - m_new).astype(BF16)
        v_t = values_t(vs_ref, c * (kc_len // LANES), kc_len // LANES)
        return m_new, jnp.exp2(m_old - m_new) * acc_old + _dot(v_t, p)

    def chunk_pair(j, carry):
        slc_scores(2 * j + 1, 1)
        m, acc = slc_update(2 * j, 0, m_ref[...], acc_ref[...])
        slc_scores(jnp.minimum(2 * j + 2, last), 0)
        m, acc = slc_update(2 * j + 1, 1, m, acc)
        m_ref[...] = m
        acc_ref[...] = acc
        return carry

    m_ref[...] = jnp.full(m_ref.shape, NEG, F32)
    acc_ref[...] = jnp.zeros_like(acc_ref)
    slc_scores(0, 0)
    lax.fori_loop(0, (last + 1) // 2, chunk_pair, 0)

    @pl.when(last % 2 == 0)
    def _():
        m, acc = slc_update(last, 0, m_ref[...], acc_ref[...])
        m_ref[...] = m
        acc_ref[...] = acc

    o_slc = _normalize(acc_ref[...])

    gates = gate_ref[0, 0]
    head_lane = lax.broadcasted_iota(jnp.int32, (V_ROWS, qb), 0) < NSA_DH
    no_lanes = jnp.zeros((pitch - V_ROWS, qb), F32)
    outs = []
    for r in range(rep):
        cols = slice(r * qb, (r + 1) * qb)
        o_r = (gates[3 * r:3 * r + 1] * o_cmp[:, cols] + gates[3 * r + 1:3 * r + 2] * o_slc[:, cols]
               + gates[3 * r + 2:3 * r + 3] * o_win[:, cols])
        outs.append(jnp.concatenate([jnp.where(head_lane, o_r, 0.0), no_lanes], axis=0).T)
    o_ref[...] = jnp.concatenate(outs, axis=1).astype(o_ref.dtype)


def _nsa_attention(kvq, gates_t, k_cmp, v_cmp_t, vs_t, vw_t, batch, seq):
    t = kvq.shape[0]
    g, rep, pitch = NSA_GROUPS, NSA_REP, HEAD_PITCH
    nq = seq // Q_BLOCK
    n_cmp_pad = k_cmp.shape[2]
    nsel = seq // SLC_BLOCK
    nsel_pad = max(LANES, nsel)
    ci = np.arange(n_cmp_pad)[None, :]
    sj = np.arange(nsel_pad)[:, None]
    overlap_t = ((ci * CMP_STRIDE < (sj + 1) * SLC_BLOCK) & (ci * CMP_STRIDE + CMP_LEN > sj * SLC_BLOCK)
                 & (ci < n_cmp_pad - 1) & (sj < nsel))
    ovl = jnp.asarray(np.concatenate([overlap_t, np.ones((8, n_cmp_pad), bool)]), BF16)
    kc_len = min(SLC_CHUNK, seq)
    key_block = jnp.asarray((np.arange(seq) // SLC_BLOCK)[:, None] == np.arange(nsel_pad)[None, :], BF16)
    kern = functools.partial(_nsa_attn_kernel, seq=seq, top_k=min(SLC_TOPK, nsel))
    per_bg = lambda b, gg, i: (b, gg, 0, 0)
    fixed = lambda b, gg, i: (0, 0)

    def keys(idx):
        return pl.BlockSpec((seq, pitch), lambda b, gg, i: (b, idx * g + gg))

    values = pl.BlockSpec((1, 1, seq // LANES, pitch, LANES), lambda b, gg, i: (b, gg, 0, 0, 0))
    return pl.pallas_call(
        kern,
        grid=(batch, g, nq),
        in_specs=[
            pl.BlockSpec((Q_BLOCK, rep * pitch), lambda b, gg, i: (b * nq + i, 6 + gg)),
            pl.BlockSpec((1, 1, 3 * rep, Q_BLOCK), lambda b, gg, i: (b, gg, 0, i)),
            pl.BlockSpec((1, 1, n_cmp_pad, pitch), per_bg),
            pl.BlockSpec((1, 1, pitch, n_cmp_pad), per_bg),
            keys(2), values, keys(4), values,
            pl.BlockSpec(ovl.shape, fixed),
            pl.BlockSpec(key_block.shape, fixed),
        ],
        out_specs=pl.BlockSpec((Q_BLOCK, rep * pitch), lambda b, gg, i: (b * nq + i, gg)),
        out_shape=jax.ShapeDtypeStruct((t, NSA_HEADS * pitch), BF16),
        scratch_shapes=[pltpu.VMEM((1, rep * Q_BLOCK), F32), pltpu.VMEM((V_ROWS, rep * Q_BLOCK), F32),
                        pltpu.VMEM((kc_len, rep * Q_BLOCK), F32), pltpu.VMEM((kc_len, rep * Q_BLOCK), F32)],
        compiler_params=_cparams(("parallel", "parallel", "arbitrary")),
        name="nsa_attention",
    )(kvq, gates_t, k_cmp, v_cmp_t, kvq, vs_t, kvq, vw_t, ovl, key_block)


def _moe_layer(h1x, cls, wg, wu, wd, layer, ln_g, ln_b):
    tile_ea, tile_eb, row_dst, n_used = _moe_plan(cls[0])
    return _experts(h1x, tile_ea, tile_eb, row_dst, n_used, wg, wu, wd, layer, ln_g, ln_b, h1x.shape[0])


def kernel(x, ret_w_in, ret_w_out, nsa_w_kv, cmp_pe_k, cmp_pe_v, cmp_k_w1, cmp_k_w2, cmp_v_w1, cmp_v_w2,
           nsa_w_in, nsa_w_out, router_w, router_b, moe_w_gate, moe_w_up, moe_w_down,
           ln_mix_g, ln_mix_b, ln_ffn_g, ln_ffn_b):
    batch, seq, d = x.shape
    t = batch * seq
    h = x.reshape(t, d)

    rwt = router_w.T
    rwt_hi = rwt.astype(BF16)
    rwt_lo = (rwt - rwt_hi.astype(F32)).astype(BF16)
    rb = router_b.reshape(N_EXPERTS, 1).astype(F32)
    row = lambda v: v.reshape(1, d)

    proj = _ret_proj(h, ret_w_in[0].astype(BF16), seq)
    gated = _ret_core(proj, batch, seq)
    h1x, cls = _mix_out(gated, ret_w_out[0].astype(BF16), h, row(ln_mix_g[0]), row(ln_mix_b[0]),
                        rwt_hi, rwt_lo, rb)
    h = _moe_layer(h1x, cls, moe_w_gate, moe_w_up, moe_w_down, 0,
                   row(ln_ffn_g[0]), row(ln_ffn_b[0]))

    g, dh, rep, pitch = NSA_GROUPS, NSA_DH, NSA_REP, HEAD_PITCH
    n_q = NSA_HEADS * dh

    def pitch_cols(w):
        heads = w.shape[-1] // dh
        lead = w.shape[:-1]
        padded = jnp.pad(w.reshape(lead + (heads, dh)), [(0, 0)] * (len(lead) + 1) + [(0, pitch - dh)])
        return padded.reshape(lead + (heads * pitch,))

    kv_parts = [nsa_w_kv[:, p * KV_W:(p + 1) * KV_W] for p in range(6)]
    w_all = jnp.concatenate(
        [_split_halves(w, dh, pitch) if p % 2 == 0 else pitch_cols(w) for p, w in enumerate(kv_parts)]
        + [_split_halves(nsa_w_in[0][:, :n_q], dh, pitch)], axis=1).astype(BF16)
    kvq = _nsa_proj(h, w_all, seq, t)
    w_gate = jnp.pad(nsa_w_in[0][:, n_q:], ((0, 0), (0, LANES - 3 * NSA_HEADS))).astype(BF16)
    gates = _nsa_gates(h, w_gate, t)[:, :3 * NSA_HEADS]
    gates_t = gates.reshape(batch, seq, g, 3 * rep).transpose(0, 2, 3, 1)

    def blocks16(p, split):
        p = p.reshape(batch, seq // CMP_STRIDE, CMP_STRIDE, g, pitch)
        if split:
            p = jnp.concatenate([p[..., :dh // 2], p[..., pitch // 2:pitch // 2 + dh // 2]], axis=-1)
        else:
            p = p[..., :dh]
        return p.transpose(0, 3, 1, 2, 4).reshape(batch, g, seq // CMP_STRIDE, CMP_STRIDE * dh)

    gw = g * pitch
    xkv = jnp.stack([blocks16(kvq[:, :gw], True), blocks16(kvq[:, gw:2 * gw], False)])
    half_w = CMP_STRIDE * dh
    pe = jnp.stack([cmp_pe_k.reshape(2, half_w), cmp_pe_v.reshape(2, half_w)]).astype(F32)
    w1 = jnp.stack([cmp_k_w1.reshape(2, half_w, CMP_HIDDEN), cmp_v_w1.reshape(2, half_w, CMP_HIDDEN)]).astype(BF16)
    w2 = jnp.stack([_split_halves(cmp_k_w2, dh, pitch), pitch_cols(cmp_v_w2)]).astype(BF16)
    kv_cmp = _compress(xkv, pe, w1, w2)

    def keys_on_lanes(p):
        return p.reshape(batch, seq // LANES, LANES, g, pitch).transpose(0, 3, 1, 4, 2)

    attn = _nsa_attention(kvq, gates_t, kv_cmp[0], kv_cmp[1].transpose(0, 1, 3, 2),
                          keys_on_lanes(kvq[:, 3 * gw:4 * gw]), keys_on_lanes(kvq[:, 5 * gw:6 * gw]),
                          batch, seq)
    w_out = jnp.pad(nsa_w_out[0].reshape(NSA_HEADS, dh, d), ((0, 0), (0, pitch - dh), (0, 0))).reshape(
        NSA_HEADS * pitch, d).astype(BF16)
    h1x, cls = _mix_out(attn, w_out, h, row(ln_mix_g[1]), row(ln_mix_b[1]), rwt_hi, rwt_lo, rb)
    h = _moe_layer(h1x, cls, moe_w_gate, moe_w_up, moe_w_down, 1, row(ln_ffn_g[1]), row(ln_ffn_b[1]))
    return h[:t].reshape(batch, seq, d)
```
